```python
import jax, jax.numpy as jnp
from jax import lax
import numpy as np

D_MODEL = 2048
BATCH = 4
SEQ = 2048
DEPTH = 2
DEC_BATCH = 128
DEC_SEQ = 1
PAST_LEN = 16384
PAGE_SIZE = 128

W_A = D_MODEL // 4
POOL_WINDOWS = (2, 4, 8, 16)
N_POOL_GROUPS = len(POOL_WINDOWS)
POOL_GROUP = W_A // N_POOL_GROUPS
POOL_BUF = max(POOL_WINDOWS) - 1
W_B = 3 * D_MODEL // 8
CHUNK = 128
GROUP_B = 128
H_B = W_B // GROUP_B
W_C = 3 * D_MODEL // 8
CONV_WIDTH = 31
CONV_BUF = CONV_WIDTH - 1
N_BRANCH = 3
N_IN = W_A + 2 * W_B + 2 * W_C + N_BRANCH * D_MODEL
N_EXPERTS = 32
TOP_K = 4
D_FF = D_MODEL
SWIGLU_LIMIT = 7.0
SWIGLU_ALPHA = 1.702
EPS = 1e-6

kernel_name = 'hybrid_pool_gmlp_conformer_moe_decoder_step'


def rmsnorm(x, g):
    xf = x.astype(jnp.float32)
    y = xf * lax.rsqrt(jnp.mean(xf * xf, axis=-1, keepdims=True) + EPS)
    return (y * g.astype(jnp.float32)).astype(x.dtype)


def layernorm(x, g, b):
    xf = x.astype(jnp.float32)
    mu = jnp.mean(xf, axis=-1, keepdims=True)
    xc = xf - mu
    y = xc * lax.rsqrt(jnp.mean(xc * xc, axis=-1, keepdims=True) + EPS)
    return (y * g.astype(jnp.float32) + b.astype(jnp.float32)).astype(x.dtype)


def pool_mix(a, prefix, start_pos, w_pool, b_pool, scale):
    n, t, _ = a.shape
    full = jnp.concatenate([prefix, a], axis=1)
    ff = full.astype(jnp.float32)
    cs = jnp.concatenate([jnp.zeros((n, 1, W_A), jnp.float32), jnp.cumsum(ff, axis=1)], axis=1)
    hi = cs[:, POOL_BUF + 1:]
    pos = start_pos + jnp.arange(t)
    means = []
    for g, w in enumerate(POOL_WINDOWS):
        sl = slice(g * POOL_GROUP, (g + 1) * POOL_GROUP)
        lo = cs[:, POOL_BUF + 1 - w:POOL_BUF + 1 - w + t, sl]
        cnt = jnp.minimum(w, pos + 1).astype(jnp.float32)[None, :, None]
        means.append((hi[..., sl] - lo) / cnt)
    mean = jnp.concatenate(means, axis=-1)
    d = (mean - ff[:, POOL_BUF:]).astype(a.dtype).reshape(n, t, N_POOL_GROUPS, POOL_GROUP)
    out = jnp.einsum('ntgc,gcd->ntgd', d, w_pool).reshape(n, t, W_A)
    return (out + b_pool) * scale, full[:, -POOL_BUF:]


def spatial_gate(v, w_s, b_s):
    n, t, _ = v.shape
    tp = -(-t // CHUNK) * CHUNK
    vp = jnp.pad(v, ((0, 0), (0, tp - t), (0, 0))).reshape(n, tp // CHUNK, CHUNK, H_B, GROUP_B)
    mask = jnp.tril(jnp.ones((CHUNK, CHUNK), dtype=bool))
    ws = jnp.where(mask[None], w_s, 0)
    s = jnp.einsum('gij,ncjgd->ncigd', ws, vp) + b_s.T[None, None, :, :, None]
    return s.reshape(n, tp, W_B)[:, :t]


def causal_depthwise(g, prefix, w_dw, b_dw):
    full = jnp.concatenate([prefix, g], axis=1)
    out = lax.conv_general_dilated(full, w_dw[:, None, :], (1,), 'VALID',
                                   dimension_numbers=('NWC', 'WIO', 'NWC'),
                                   feature_group_count=W_C)
    return out + b_dw, full[:, -CONV_BUF:]


def moe(h, w_router, b_router, w_up, b_up, w_down, b_down):
    n, t, d = h.shape
    x = h.reshape(n * t, d)
    logits = (x @ w_router + b_router).astype(jnp.float32)
    top_val, top_idx = lax.top_k(logits, TOP_K)
    top_w = jax.nn.softmax(top_val, axis=-1)
    combine = jnp.einsum('mk,mke->me', top_w,
                         jax.nn.one_hot(top_idx, N_EXPERTS, dtype=jnp.float32)).astype(x.dtype)
    y = jnp.zeros_like(x)
    for e in range(N_EXPERTS):
        gu = x @ w_up[e] + b_up[e]
        gate = jnp.minimum(gu[:, :D_FF], SWIGLU_LIMIT)
        up = jnp.clip(gu[:, D_FF:], -SWIGLU_LIMIT, SWIGLU_LIMIT)
        act = gate * jax.nn.sigmoid(SWIGLU_ALPHA * gate) * (up + 1)
        y = y + combine[:, e:e + 1] * (act @ w_down[e] + b_down[e])
    return y.reshape(n, t, d)


def decoder_layer(x, c, pool_prefix, conv_prefix, start_pos, p):
    mod = jax.nn.silu(c) @ p['w_ada'] + p['b_ada']
    shift1, scale1, gate1, shift2, scale2, gate2 = jnp.split(mod[:, None, :], 6, axis=-1)
    h = rmsnorm(x, p['g_pre1']) * (1 + scale1) + shift1
    z = h @ p['w_in'] + p['b_in']
    a, uv, cg, gz = jnp.split(z, [W_A, W_A + 2 * W_B, W_A + 2 * W_B + 2 * W_C], axis=-1)
    pa, new_pool = pool_mix(a, pool_prefix, start_pos, p['w_pool'], p['b_pool'], p['pool_scale'])
    u, v = jnp.split(jax.nn.gelu(uv), 2, axis=-1)
    v = layernorm(v, p['ln_v_g'], p['ln_v_b'])
    sb = u * spatial_gate(v, p['w_s'], p['b_s'])
    ga, gb = jnp.split(cg, 2, axis=-1)
    glu = ga * jax.nn.sigmoid(gb)
    cv, new_conv = causal_depthwise(glu, conv_prefix, p['w_dw'], p['b_dw'])
    sc = jax.nn.silu(layernorm(cv, p['ln_c_g'], p['ln_c_b']))
    g_a, g_b, g_c = jnp.split(jax.nn.sigmoid(gz), 3, axis=-1)
    merged = g_a * (pa @ p['w_pa']) + g_b * (sb @ p['w_pb']) + g_c * (sc @ p['w_pc'])
    x = x + gate1 * rmsnorm(merged @ p['w_o'], p['g_post1'])
    h2 = rmsnorm(x, p['g_pre2']) * (1 + scale2) + shift2
    m = moe(h2, p['w_router'], p['b_router'], p['w_up'], p['b_up'], p['w_down'], p['b_down'])
    x = x + gate2 * rmsnorm(m, p['g_post2'])
    return x, new_pool, new_conv, v


def setup_inputs(seed: int = 0) -> dict:
    key = jax.random.key(seed)
    ks = jax.random.split(key, 48)
    ctr = [0]

    def nrm(shape, scale):
        k = ks[ctr[0]]
        ctr[0] += 1
        return jax.random.normal(k, shape, jnp.float32) * scale

    def gain(shape):
        return 1.0 + nrm(shape, 0.05)

    L, D = DEPTH, D_MODEL
    return {
        'x_prompt': nrm((BATCH, SEQ, D), 1.0),
        'x_sample': nrm((DEC_BATCH, DEC_SEQ, D), 1.0),
        'c_prompt': nrm((BATCH, D), 1.0),
        'c_sample': nrm((DEC_BATCH, D), 1.0),
        'state_pool': nrm((L, DEC_BATCH, POOL_BUF, W_A), 1.0),
        'state_conv': nrm((L, DEC_BATCH, CONV_BUF, W_C), 0.5),
        'w_ada': nrm((L, D, 6 * D), 0.5 * D ** -0.5),
        'b_ada': nrm((L, 6 * D), 0.02),
        'g_pre1': gain((L, D)),
        'g_post1': gain((L, D)),
        'g_pre2': gain((L, D)),
        'g_post2': gain((L, D)),
        'w_in': nrm((L, D, N_IN), D ** -0.5),
        'b_in': nrm((L, N_IN), 0.02),
        'w_pool': nrm((L, N_POOL_GROUPS, POOL_GROUP, POOL_GROUP), POOL_GROUP ** -0.5),
        'b_pool': nrm((L, W_A), 0.02),
        'pool_scale': gain((L, W_A)),
        'ln_v_g': gain((L, W_B)),
        'ln_v_b': nrm((L, W_B), 0.02),
        'w_s': nrm((L, H_B, CHUNK, CHUNK), CHUNK ** -0.5),
        'b_s': nrm((L, H_B, CHUNK), 0.02),
        'w_dw': nrm((L, CONV_WIDTH, W_C), CONV_WIDTH ** -0.5),
        'b_dw': nrm((L, W_C), 0.02),
        'ln_c_g': gain((L, W_C)),
        'ln_c_b': nrm((L, W_C), 0.02),
        'w_pa': nrm((L, W_A, D), W_A ** -0.5),
        'w_pb': nrm((L, W_B, D), W_B ** -0.5),
        'w_pc': nrm((L, W_C, D), W_C ** -0.5),
        'w_o': nrm((L, D, D), D ** -0.5),
        'w_router': nrm((L, D, N_EXPERTS), D ** -0.5),
        'b_router': nrm((L, N_EXPERTS), 0.01),
        'w_up': nrm((L, N_EXPERTS, D, 2 * D_FF), D ** -0.5),
        'b_up': nrm((L, N_EXPERTS, 2 * D_FF), 0.01),
        'w_down': nrm((L, N_EXPERTS, D_FF, D), D_FF ** -0.5),
        'b_down': nrm((L, N_EXPERTS, D), 0.01),
    }


def reference(x_prompt, x_sample, c_prompt, c_sample, state_pool, state_conv,
              w_ada, b_ada, g_pre1, g_post1, g_pre2, g_post2, w_in, b_in,
              w_pool, b_pool, pool_scale, ln_v_g, ln_v_b, w_s, b_s,
              w_dw, b_dw, ln_c_g, ln_c_b, w_pa, w_pb, w_pc, w_o,
              w_router, b_router, w_up, b_up, w_down, b_down):
    y_p, y_s = x_prompt, x_sample
    n_p = x_prompt.shape[0]
    zero_pool = jnp.zeros((n_p, POOL_BUF, W_A), x_prompt.dtype)
    zero_conv = jnp.zeros((n_p, CONV_BUF, W_C), x_prompt.dtype)
    pool_p, pool_s, conv_p, conv_s, v_s = [], [], [], [], []
    for l in range(DEPTH):
        p = dict(w_ada=w_ada[l], b_ada=b_ada[l], g_pre1=g_pre1[l], g_post1=g_post1[l],
                 g_pre2=g_pre2[l], g_post2=g_post2[l], w_in=w_in[l], b_in=b_in[l],
                 w_pool=w_pool[l], b_pool=b_pool[l], pool_scale=pool_scale[l],
                 ln_v_g=ln_v_g[l], ln_v_b=ln_v_b[l], w_s=w_s[l], b_s=b_s[l],
                 w_dw=w_dw[l], b_dw=b_dw[l], ln_c_g=ln_c_g[l], ln_c_b=ln_c_b[l],
                 w_pa=w_pa[l], w_pb=w_pb[l], w_pc=w_pc[l], w_o=w_o[l],
                 w_router=w_router[l], b_router=b_router[l], w_up=w_up[l], b_up=b_up[l],
                 w_down=w_down[l], b_down=b_down[l])
        y_p, np_pool, np_conv, _ = decoder_layer(y_p, c_prompt, zero_pool, zero_conv, 0, p)
        y_s, ns_pool, ns_conv, ns_v = decoder_layer(y_s, c_sample, state_pool[l], state_conv[l], PAST_LEN, p)
        pool_p.append(np_pool)
        pool_s.append(ns_pool)
        conv_p.append(np_conv)
        conv_s.append(ns_conv)
        v_s.append(ns_v)
    new_pool_prompt = jnp.stack(pool_p)
    new_pool_sample = jnp.stack(pool_s)
    new_conv_prompt = jnp.stack(conv_p)
    new_conv_sample = jnp.stack(conv_s)
    new_v_sample = jnp.stack(v_s)
    return (y_p, y_s, new_pool_prompt, new_pool_sample, new_conv_prompt, new_conv_sample, new_v_sample)
```

```python
import functools

import jax
import jax.numpy as jnp
from jax import lax
from jax.experimental import pallas as pl
from jax.experimental.pallas import tpu as pltpu

F32 = jnp.float32
BF16 = jnp.bfloat16
I32 = jnp.int32

D_MODEL = 2048
BATCH = 4
SEQ = 2048
DEPTH = 2
DEC_BATCH = 128
PAST_LEN = 16384
W_A = 512
POOL_WINDOWS = (2, 4, 8, 16)
POOL_GROUP = 128
POOL_BUF = 15
W_B = 768
CHUNK = 128
H_B = 6
W_C = 768
CONV_WIDTH = 31
CONV_BUF = 30
N_GATES = 3 * D_MODEL
N_REST = W_A + 2 * W_B + 2 * W_C
N_IN = N_GATES + N_REST
N_EXPERTS = 32
TOP_K = 4
D_FF = D_MODEL
SWIGLU_LIMIT = 7.0
SWIGLU_ALPHA = 1.702
EPS = 1e-6

N_PROMPT = BATCH * SEQ
N_TOK = N_PROMPT + DEC_BATCH
N_MOD = DEC_BATCH + BATCH
N_MOD_PAD = 136

VMEM_LIMIT = 56 * 1024 * 1024

TN_ADA = 1024
TM_IN = 1024
TN_IN = 512
TT_MIX = 256
POOL_HALO = 16
CONV_HALO = 32
TM_MERGE = 256
TOK_TILE = 128
R_BLK = 1280
R_SUB = 256
F_UP = 256
F_DOWN = 256
NC_UP = D_FF // F_UP
NC_DOWN = D_MODEL // F_DOWN
N_BLK = N_EXPERTS + (N_TOK * TOP_K) // R_BLK


def _cparams(*sem):
    return pltpu.CompilerParams(dimension_semantics=sem, vmem_limit_bytes=VMEM_LIMIT)


def _sigmoid(x):
    return 1.0 / (1.0 + jnp.exp(-x))


def _gelu_tanh(x):
    c = 0.7978845608028654
    return x * (0.5 * (1.0 + jnp.tanh(c * (x + 0.044715 * (x * x * x)))))


def _rms(x):
    return x * lax.rsqrt(jnp.mean(x * x, axis=-1, keepdims=True) + EPS)


def _layernorm(x, g, b):
    mu = jnp.mean(x, axis=-1, keepdims=True)
    xc = x - mu
    return xc * lax.rsqrt(jnp.mean(xc * xc, axis=-1, keepdims=True) + EPS) * g + b


def _dot(a, b):
    return jnp.dot(a, b, preferred_element_type=F32)


def _ada_kernel(c_ref, w_ref, b_ref, o_ref):
    c = c_ref[...]
    s = (c * _sigmoid(c)).astype(BF16)
    o_ref[...] = _dot(s, w_ref[0].astype(BF16)) + b_ref[0]


def _ada(c_all, w_ada, b_ada3, layer):
    n = 6 * D_MODEL
    return pl.pallas_call(
        _ada_kernel,
        grid=(n // TN_ADA,),
        in_specs=[
            pl.BlockSpec((N_MOD_PAD, D_MODEL), lambda j: (0, 0)),
            pl.BlockSpec((1, D_MODEL, TN_ADA), lambda j: (layer, 0, j)),
            pl.BlockSpec((1, 1, TN_ADA), lambda j: (layer, 0, j)),
        ],
        out_specs=pl.BlockSpec((N_MOD_PAD, TN_ADA), lambda j: (0, j)),
        out_shape=jax.ShapeDtypeStruct((N_MOD_PAD, n), F32),
        compiler_params=_cparams("arbitrary"),
        name="ada",
    )(c_all, w_ada, b_ada3)


def _inproj_kernel(x_ref, sc_ref, sh_ref, g_ref, w_ref, b_ref, gates_ref, z_ref, h_scr):
    j = pl.program_id(1)

    @pl.when(j == 0)
    def _():
        y = _rms(x_ref[...]) * g_ref[...]
        h_scr[...] = (y * (1.0 + sc_ref[0]) + sh_ref[0]).astype(BF16)

    z = _dot(h_scr[...], w_ref[...]) + b_ref[...]

    @pl.when(j < N_GATES // TN_IN)
    def _():
        gates_ref[...] = _sigmoid(z).astype(BF16)

    @pl.when(j >= N_GATES // TN_IN)
    def _():
        z_ref[...] = z


def _inproj(x, mod3, g_pre, w_in_b, b_in, tm, rows_per_mod):
    m = x.shape[0]
    r = mod3.shape[1]
    n_gate_tiles = N_GATES // TN_IN
    mod_row = lambda i: i // (rows_per_mod // tm)
    return pl.pallas_call(
        _inproj_kernel,
        grid=(m // tm, N_IN // TN_IN),
        in_specs=[
            pl.BlockSpec((tm, D_MODEL), lambda i, j: (i, 0)),
            pl.BlockSpec((1, r, D_MODEL), lambda i, j: (mod_row(i), 0, 1)),
            pl.BlockSpec((1, r, D_MODEL), lambda i, j: (mod_row(i), 0, 0)),
            pl.BlockSpec((1, D_MODEL), lambda i, j: (0, 0)),
            pl.BlockSpec((D_MODEL, TN_IN), lambda i, j: (0, j)),
            pl.BlockSpec((1, TN_IN), lambda i, j: (0, j)),
        ],
        out_specs=[
            pl.BlockSpec((tm, TN_IN), lambda i, j: (i, jnp.minimum(j, n_gate_tiles - 1))),
            pl.BlockSpec((tm, TN_IN), lambda i, j: (i, jnp.maximum(j - n_gate_tiles, 0))),
        ],
        out_shape=[
            jax.ShapeDtypeStruct((m, N_GATES), BF16),
            jax.ShapeDtypeStruct((m, N_REST), F32),
        ],
        scratch_shapes=[pltpu.VMEM((tm, D_MODEL), BF16)],
        compiler_params=_cparams("arbitrary", "arbitrary"),
        name="inproj",
    )(x, mod3, mod3, g_pre, w_in_b, b_in)


def _mixer_prompt_kernel(uv_ref, cg_ref, a_ref, wpool_ref, bpool_ref, pscale_ref,
                         lvg_ref, lvb_ref, ws_ref, bs_ref, wdw_ref, bdw_ref, lcg_ref, lcb_ref,
                         pa_ref, sb_ref, sc_ref, npool_ref, nconv_ref,
                         aext, gext, cv_scr):
    t = pl.program_id(1)
    tt = TT_MIX

    @pl.when(t == 0)
    def _():
        aext[0:POOL_HALO, :] = jnp.zeros((POOL_HALO, W_A), F32)
        gext[0:CONV_HALO, :] = jnp.zeros((CONV_HALO, W_C), F32)

    a = a_ref[...]
    aext[POOL_HALO:POOL_HALO + tt, :] = a
    pos = t * tt + lax.broadcasted_iota(I32, (tt, 1), 0)
    for g, w in enumerate(POOL_WINDOWS):
        c0, c1 = g * POOL_GROUP, (g + 1) * POOL_GROUP
        cur = a[:, c0:c1]
        s = cur
        for j in range(1, w):
            s = s + aext[POOL_HALO - j:POOL_HALO - j + tt, c0:c1]
        cnt = jnp.minimum(w, pos + 1).astype(F32)
        d = (s / cnt - cur).astype(BF16)
        og = _dot(d, wpool_ref[g])
        pa_ref[:, c0:c1] = ((og + bpool_ref[:, c0:c1]) * pscale_ref[:, c0:c1]).astype(BF16)
    tail_a = aext[tt:tt + POOL_HALO, :]
    npool_ref[0] = tail_a
    aext[0:POOL_HALO, :] = tail_a

    uv = _gelu_tanh(uv_ref[...])
    u = uv[:, :W_B]
    vn = _layernorm(uv[:, W_B:], lvg_ref[...], lvb_ref[...]).astype(BF16)
    ri = lax.broadcasted_iota(I32, (CHUNK, CHUNK), 0)
    ci = lax.broadcasted_iota(I32, (CHUNK, CHUNK), 1)
    for h in range(H_B):
        h0, h1 = h * CHUNK, (h + 1) * CHUNK
        wsm = jnp.where(ri >= ci, ws_ref[h], 0.0).astype(BF16)
        for c in range(tt // CHUNK):
            r0, r1 = c * CHUNK, (c + 1) * CHUNK
            s = _dot(wsm, vn[r0:r1, h0:h1]) + bs_ref[:, h0:h1]
            sb_ref[r0:r1, h0:h1] = (u[r0:r1, h0:h1] * s).astype(BF16)

    cg = cg_ref[...]
    gext[CONV_HALO:CONV_HALO + tt, :] = cg[:, :W_C] * _sigmoid(cg[:, W_C:])
    rb = 64
    for cb in range(W_C // 128):
        c0, c1 = cb * 128, (cb + 1) * 128
        for r in range(tt // rb):
            base = CONV_HALO - CONV_BUF + r * rb
            acc = gext[base:base + rb, c0:c1] * wdw_ref[0:1, c0:c1]
            for k in range(1, CONV_WIDTH):
                acc = acc + gext[base + k:base + k + rb, c0:c1] * wdw_ref[k:k + 1, c0:c1]
            cv_scr[r * rb:(r + 1) * rb, c0:c1] = acc
    cv = _layernorm(cv_scr[...] + bdw_ref[...], lcg_ref[...], lcb_ref[...])
    sc_ref[...] = (cv * _sigmoid(cv)).astype(BF16)
    tail_g = gext[tt:tt + CONV_HALO, :]
    nconv_ref[0] = tail_g
    gext[0:CONV_HALO, :] = tail_g


def _mixer_prompt(zrest, p):
    nt = SEQ // TT_MIX
    row = lambda b, t: b * nt + t
    vec = lambda n: pl.BlockSpec((1, n), lambda b, t: (0, 0))
    return pl.pallas_call(
        _mixer_prompt_kernel,
        grid=(BATCH, nt),
        in_specs=[
            pl.BlockSpec((TT_MIX, 2 * W_B), lambda b, t: (row(b, t), 0)),
            pl.BlockSpec((TT_MIX, 2 * W_C), lambda b, t: (row(b, t), 1)),
            pl.BlockSpec((TT_MIX, W_A), lambda b, t: (row(b, t), (2 * W_B + 2 * W_C) // W_A)),
            pl.BlockSpec((len(POOL_WINDOWS), POOL_GROUP, POOL_GROUP), lambda b, t: (0, 0, 0)),
            vec(W_A), vec(W_A), vec(W_B), vec(W_B),
            pl.BlockSpec((H_B, CHUNK, CHUNK), lambda b, t: (0, 0, 0)),
            pl.BlockSpec((CHUNK, W_B), lambda b, t: (0, 0)),
            pl.BlockSpec((CONV_WIDTH, W_C), lambda b, t: (0, 0)),
            vec(W_C), vec(W_C), vec(W_C),
        ],
        out_specs=[
            pl.BlockSpec((TT_MIX, W_A), lambda b, t: (row(b, t), 0)),
            pl.BlockSpec((TT_MIX, W_B), lambda b, t: (row(b, t), 0)),
            pl.BlockSpec((TT_MIX, W_C), lambda b, t: (row(b, t), 0)),
            pl.BlockSpec((1, POOL_HALO, W_A), lambda b, t: (b, 0, 0)),
            pl.BlockSpec((1, CONV_HALO, W_C), lambda b, t: (b, 0, 0)),
        ],
        out_shape=[
            jax.ShapeDtypeStruct((N_PROMPT, W_A), BF16),
            jax.ShapeDtypeStruct((N_PROMPT, W_B), BF16),
            jax.ShapeDtypeStruct((N_PROMPT, W_C), BF16),
            jax.ShapeDtypeStruct((BATCH, POOL_HALO, W_A), F32),
            jax.ShapeDtypeStruct((BATCH, CONV_HALO, W_C), F32),
        ],
        scratch_shapes=[
            pltpu.VMEM((TT_MIX + POOL_HALO, W_A), F32),
            pltpu.VMEM((TT_MIX + CONV_HALO, W_C), F32),
            pltpu.VMEM((TT_MIX, W_C), F32),
        ],
        compiler_params=_cparams("arbitrary", "arbitrary"),
        name="mixer_prompt",
    )(zrest, zrest, zrest, p["w_pool_b"], p["b_pool"], p["pool_scale"], p["ln_v_g"], p["ln_v_b"],
      p["w_s"], p["bs_full"], p["w_dw"], p["b_dw"], p["ln_c_g"], p["ln_c_b"])


def _mixer_sample_kernel(uv_ref, cg_ref, a_ref, spool_ref, sconv_ref, wpool_ref, bpool_ref,
                         pscale_ref, lvg_ref, lvb_ref, ws0_ref, bs0_ref, wdw_ref, bdw_ref,
                         lcg_ref, lcb_ref, pa_ref, sb_ref, sc_ref, glu_ref, v_ref):
    a = a_ref[...]
    for g, w in enumerate(POOL_WINDOWS):
        c0, c1 = g * POOL_GROUP, (g + 1) * POOL_GROUP
        cur = a[:, c0:c1]
        s = cur
        for j in range(1, w):
            s = s + spool_ref[POOL_BUF - j, :, c0:c1]
        cnt = float(min(w, PAST_LEN + 1))
        d = (s / cnt - cur).astype(BF16)
        og = _dot(d, wpool_ref[g])
        pa_ref[:, c0:c1] = ((og + bpool_ref[:, c0:c1]) * pscale_ref[:, c0:c1]).astype(BF16)

    uv = _gelu_tanh(uv_ref[...])
    vn = _layernorm(uv[:, W_B:], lvg_ref[...], lvb_ref[...])
    v_ref[...] = vn
    sb_ref[...] = (uv[:, :W_B] * (ws0_ref[...] * vn + bs0_ref[...])).astype(BF16)

    cg = cg_ref[...]
    glu = cg[:, :W_C] * _sigmoid(cg[:, W_C:])
    glu_ref[...] = glu
    acc = glu * wdw_ref[CONV_BUF:CONV_BUF + 1, :]
    for k in range(CONV_BUF):
        acc = acc + sconv_ref[k] * wdw_ref[k:k + 1, :]
    cv = _layernorm(acc + bdw_ref[...], lcg_ref[...], lcb_ref[...])
    sc_ref[...] = (cv * _sigmoid(cv)).astype(BF16)


def _mixer_sample(zrest, spool_t, sconv_t, p):
    n = DEC_BATCH
    vec = lambda w: pl.BlockSpec((1, w), lambda i: (0, 0))
    return pl.pallas_call(
        _mixer_sample_kernel,
        grid=(1,),
        in_specs=[
            pl.BlockSpec((n, 2 * W_B), lambda i: (0, 0)),
            pl.BlockSpec((n, 2 * W_C), lambda i: (0, 1)),
            pl.BlockSpec((n, W_A), lambda i: (0, (2 * W_B + 2 * W_C) // W_A)),
            pl.BlockSpec((POOL_BUF, n, W_A), lambda i: (0, 0, 0)),
            pl.BlockSpec((CONV_BUF, n, W_C), lambda i: (0, 0, 0)),
            pl.BlockSpec((len(POOL_WINDOWS), POOL_GROUP, POOL_GROUP), lambda i: (0, 0, 0)),
            vec(W_A), vec(W_A), vec(W_B), vec(W_B), vec(W_B), vec(W_B),
            pl.BlockSpec((CONV_WIDTH, W_C), lambda i: (0, 0)),
            vec(W_C), vec(W_C), vec(W_C),
        ],
        out_specs=[
            pl.BlockSpec((n, W_A), lambda i: (0, 0)),
            pl.BlockSpec((n, W_B), lambda i: (0, 0)),
            pl.BlockSpec((n, W_C), lambda i: (0, 0)),
            pl.BlockSpec((n, W_C), lambda i: (0, 0)),
            pl.BlockSpec((n, W_B), lambda i: (0, 0)),
        ],
        out_shape=[
            jax.ShapeDtypeStruct((n, W_A), BF16),
            jax.ShapeDtypeStruct((n, W_B), BF16),
            jax.ShapeDtypeStruct((n, W_C), BF16),
            jax.ShapeDtypeStruct((n, W_C), F32),
            jax.ShapeDtypeStruct((n, W_B), F32),
        ],
        compiler_params=_cparams("arbitrary"),
        name="mixer_sample",
    )(zrest, zrest, zrest, spool_t, sconv_t, p["w_pool_b"], p["b_pool"], p["pool_scale"],
      p["ln_v_g"], p["ln_v_b"], p["ws0"], p["bs0"], p["w_dw"], p["b_dw"], p["ln_c_g"], p["ln_c_b"])


def _merge_kernel(pa_ref, sb_ref, sc_ref, gates_ref, x_ref, gate1_ref, scale2_ref, shift2_ref,
                  gpost1_ref, gpre2_ref, wpa_ref, wpb_ref, wpc_ref, wo_ref, wrt_ref, brt_ref,
                  cnt_in_ref, *rest):
    x1_ref, h2_ref, idx_ref, wgt_ref, rank_ref, cnt_ref, cnt_scr = rest[-7:]
    tm = x_ref.shape[0]
    i = pl.program_id(0)

    @pl.when(i == 0)
    def _():
        cnt_scr[...] = cnt_in_ref[...]

    g = gates_ref[...]
    m = g[:, :D_MODEL].astype(F32) * _dot(pa_ref[...], wpa_ref[...])
    m = m + g[:, D_MODEL:2 * D_MODEL].astype(F32) * _dot(sb_ref[...], wpb_ref[...])
    m = m + g[:, 2 * D_MODEL:].astype(F32) * _dot(sc_ref[...], wpc_ref[...])
    o = _dot(m.astype(BF16), wo_ref[...])
    x1 = x_ref[...] + gate1_ref[0] * (_rms(o) * gpost1_ref[...])
    x1_ref[...] = x1
    h2 = _rms(x1) * gpre2_ref[...] * (1.0 + scale2_ref[0]) + shift2_ref[0]
    h2_ref[...] = h2

    logits = lax.dot_general(wrt_ref[...], h2.astype(BF16), (((1,), (1,)), ((), ())),
                             preferred_element_type=F32) + brt_ref[...]
    eidx = lax.broadcasted_iota(I32, (N_EXPERTS, tm), 0).astype(F32)
    work = logits
    vals, onehots = [], []
    for k in range(TOP_K):
        mx = jnp.max(work, axis=0, keepdims=True)
        ik = jnp.min(jnp.where(work == mx, eidx, float(N_EXPERTS)), axis=0, keepdims=True)
        oh = eidx == ik
        vals.append(mx)
        onehots.append(oh)
        idx_ref[k:k + 1, :] = ik.astype(I32)
        work = jnp.where(oh, -jnp.inf, work)
    ex = [jnp.exp(v - vals[0]) for v in vals]
    den = ex[0] + ex[1] + ex[2] + ex[3]
    for k in range(TOP_K):
        wgt_ref[k:k + 1, :] = ex[k] / den

    sel = jnp.zeros((N_EXPERTS, tm), F32)
    for oh in onehots:
        sel = sel + jnp.where(oh, 1.0, 0.0)
    ri = lax.broadcasted_iota(I32, (tm, tm), 0)
    ci = lax.broadcasted_iota(I32, (tm, tm), 1)
    before = jnp.where(ri < ci, 1.0, 0.0).astype(BF16)
    base = cnt_scr[:, 0:1] + _dot(sel.astype(BF16), before)
    for k in range(TOP_K):
        rank_ref[k:k + 1, :] = jnp.sum(jnp.where(onehots[k], base, 0.0), axis=0,
                                       keepdims=True).astype(I32)
    cnt_scr[...] = cnt_scr[...] + jnp.sum(sel, axis=1, keepdims=True)
    cnt_ref[...] = cnt_scr[...]


def _merge(pa, sb, sc, gates, x, mod3, p, cnt_in, tm, rows_per_mod, h2_all, h2_tile_off):
    m = x.shape[0]
    r = mod3.shape[1]
    mod_row = lambda i: i // (rows_per_mod // tm)
    modspec = lambda k: pl.BlockSpec((1, r, D_MODEL), lambda i: (mod_row(i), 0, k))
    const = lambda shape: pl.BlockSpec(shape, lambda i: (0,) * len(shape),
                                       pipeline_mode=pl.Buffered(1))
    row = lambda w: pl.BlockSpec((tm, w), lambda i: (i, 0))
    tok = pl.BlockSpec((TOP_K, tm), lambda i: (0, i))
    in_specs = [
        row(W_A), row(W_B), row(W_C), row(N_GATES), row(D_MODEL),
        modspec(2), modspec(4), modspec(3),
        const((1, D_MODEL)), const((1, D_MODEL)),
        const((W_A, D_MODEL)), const((W_B, D_MODEL)), const((W_C, D_MODEL)),
        const((D_MODEL, D_MODEL)), const((N_EXPERTS, D_MODEL)), const((N_EXPERTS, 1)),
        const((N_EXPERTS, 128)),
    ]
    args = [pa, sb, sc, gates, x, mod3, mod3, mod3, p["g_post1"], p["g_pre2"],
            p["w_pa_b"], p["w_pb_b"], p["w_pc_b"], p["w_o_b"], p["w_rt_b"], p["b_rt"], cnt_in]
    aliases = {}
    if h2_all is not None:
        in_specs.append(pl.BlockSpec(memory_space=pl.ANY))
        args.append(h2_all)
        aliases = {len(args) - 1: 1}
    return pl.pallas_call(
        _merge_kernel,
        grid=(m // tm,),
        in_specs=in_specs,
        out_specs=[
            row(D_MODEL),
            pl.BlockSpec((tm, D_MODEL), lambda i: (i + h2_tile_off, 0)),
            tok, tok, tok,
            pl.BlockSpec((N_EXPERTS, 128), lambda i: (0, 0)),
        ],
        out_shape=[
            jax.ShapeDtypeStruct((m, D_MODEL), F32),
            jax.ShapeDtypeStruct((N_TOK, D_MODEL), F32),
            jax.ShapeDtypeStruct((TOP_K, m), I32),
            jax.ShapeDtypeStruct((TOP_K, m), F32),
            jax.ShapeDtypeStruct((TOP_K, m), I32),
            jax.ShapeDtypeStruct((N_EXPERTS, 128), F32),
        ],
        scratch_shapes=[pltpu.VMEM((N_EXPERTS, 128), F32)],
        input_output_aliases=aliases,
        compiler_params=_cparams("arbitrary"),
        name="merge_router",
    )(*args)


def _dispatch_kernel(pos_hbm, h2_ref, xs_hbm, pos_smem, sem_pos, sem_rows):
    i = pl.program_id(0)
    cp = pltpu.make_async_copy(pos_hbm.at[pl.ds(i, 1)], pos_smem, sem_pos)
    cp.start()
    cp.wait()

    def row_copy(t, k):
        dst = pos_smem[0, k * TOK_TILE + t]
        return pltpu.make_async_copy(h2_ref.at[pl.ds(t, 1)], xs_hbm.at[pl.ds(dst, 1)], sem_rows)

    def issue(t, c):
        for k in range(TOP_K):
            row_copy(t, k).start()
        return c

    lax.fori_loop(0, TOK_TILE, issue, 0)

    def drain(t, c):
        for k in range(TOP_K):
            row_copy(t, k).wait()
        return c

    lax.fori_loop(0, TOK_TILE, drain, 0)


def _dispatch(pos_tiles, h2_all):
    return pl.pallas_call(
        _dispatch_kernel,
        grid=(N_TOK // TOK_TILE,),
        in_specs=[
            pl.BlockSpec(memory_space=pl.ANY),
            pl.BlockSpec((TOK_TILE, D_MODEL), lambda i: (i, 0)),
        ],
        out_specs=pl.BlockSpec(memory_space=pl.ANY),
        out_shape=jax.ShapeDtypeStruct((N_BLK * R_BLK, D_MODEL), F32),
        scratch_shapes=[
            pltpu.SMEM((1, TOP_K * TOK_TILE), I32),
            pltpu.SemaphoreType.DMA(()),
            pltpu.SemaphoreType.DMA(()),
        ],
        compiler_params=_cparams("arbitrary"),
        name="moe_dispatch",
    )(pos_tiles, h2_all)


def _experts_kernel(eid_ref, rows_ref, valid_ref, ueff_ref,
                    x_ref, wg_ref, wu_ref, bg_ref, bu_ref, wd_ref, bd_ref, y_ref,
                    xbuf, actbuf, wgb, wub, wdb):
    del eid_ref, valid_ref, ueff_ref
    u = pl.program_id(0)
    s = pl.program_id(1)
    nrows = rows_ref[u]
    nsub = (nrows + (R_SUB - 1)) // R_SUB
    live = nrows > 0

    def sub(r):
        return pl.multiple_of(r * R_SUB, R_SUB)

    @pl.when(jnp.logical_and(live, s == 0))
    def _():
        def body(r, c):
            r0 = sub(r)
            rid = r0 + lax.broadcasted_iota(I32, (R_SUB, 1), 0)
            xs = jnp.where(rid < nrows, x_ref[pl.ds(r0, R_SUB), :], 0.0)
            xbuf[pl.ds(r0, R_SUB), :] = xs.astype(BF16)
            return c
        lax.fori_loop(0, nsub, body, 0)

    @pl.when(jnp.logical_and(live, s < NC_UP))
    def _():
        wgb[...] = wg_ref[0, 0].astype(BF16)
        wub[...] = wu_ref[0, 0].astype(BF16)
        chunk = jnp.minimum(s, NC_UP - 1)

        def body(r, c):
            r0 = sub(r)
            xs = xbuf[pl.ds(r0, R_SUB), :]
            gate = jnp.minimum(_dot(xs, wgb[...]) + bg_ref[0, 0], SWIGLU_LIMIT)
            up = jnp.clip(_dot(xs, wub[...]) + bu_ref[0, 0], -SWIGLU_LIMIT, SWIGLU_LIMIT)
            act = gate * _sigmoid(SWIGLU_ALPHA * gate) * (up + 1.0)
            actbuf[chunk, pl.ds(r0, R_SUB), :] = act.astype(BF16)
            return c
        lax.fori_loop(0, nsub, body, 0)

    @pl.when(jnp.logical_and(live, s >= NC_UP))
    def _():
        wdb[...] = wd_ref[0, 0].astype(BF16)

        def body(r, c):
            r0 = sub(r)
            acc = _dot(actbuf[0, pl.ds(r0, R_SUB), :], wdb[0:F_UP, :])
            for k in range(1, NC_UP):
                acc = acc + _dot(actbuf[k, pl.ds(r0, R_SUB), :], wdb[k * F_UP:(k + 1) * F_UP, :])
            y_ref[pl.ds(r0, R_SUB), :] = acc + bd_ref[0, 0]
            return c
        lax.fori_loop(0, nsub, body, 0)

        def zero(r, c):
            y_ref[pl.ds(sub(r), R_SUB), :] = jnp.zeros((R_SUB, F_DOWN), F32)
            return c
        lax.fori_loop(nsub, R_BLK // R_SUB, zero, 0)


def _experts(meta, xs, w_up, b_up4, w_down, b_down4, layer):
    eid, rows, valid, ueff = meta

    def up_chunk(u, s, valid):
        return jnp.where(valid[u] == 1, jnp.minimum(s, NC_UP - 1), NC_UP - 1)

    def down_chunk(u, s, valid):
        return jnp.where(valid[u] == 1, jnp.maximum(s - NC_UP, 0), NC_DOWN - 1)

    grid_spec = pltpu.PrefetchScalarGridSpec(
        num_scalar_prefetch=4,
        grid=(N_BLK, NC_UP + NC_DOWN),
        in_specs=[
            pl.BlockSpec((R_BLK, D_MODEL), lambda u, s, e, r, v, f: (f[u], 0)),
            pl.BlockSpec((1, 1, D_MODEL, F_UP),
                         lambda u, s, e, r, v, f: (layer, e[u], 0, up_chunk(u, s, v))),
            pl.BlockSpec((1, 1, D_MODEL, F_UP),
                         lambda u, s, e, r, v, f: (layer, e[u], 0, NC_UP + up_chunk(u, s, v))),
            pl.BlockSpec((1, 1, 1, F_UP),
                         lambda u, s, e, r, v, f: (layer, e[u], 0, up_chunk(u, s, v))),
            pl.BlockSpec((1, 1, 1, F_UP),
                         lambda u, s, e, r, v, f: (layer, e[u], 0, NC_UP + up_chunk(u, s, v))),
            pl.BlockSpec((1, 1, D_FF, F_DOWN),
                         lambda u, s, e, r, v, f: (layer, e[u], 0, down_chunk(u, s, v))),
            pl.BlockSpec((1, 1, 1, F_DOWN),
                         lambda u, s, e, r, v, f: (layer, e[u], 0, down_chunk(u, s, v))),
        ],
        out_specs=pl.BlockSpec((R_BLK, F_DOWN), lambda u, s, e, r, v, f: (f[u], down_chunk(u, s, v))),
        scratch_shapes=[
            pltpu.VMEM((R_BLK, D_MODEL), BF16),
            pltpu.VMEM((NC_UP, R_BLK, F_UP), BF16),
            pltpu.VMEM((D_MODEL, F_UP), BF16),
            pltpu.VMEM((D_MODEL, F_UP), BF16),
            pltpu.VMEM((D_FF, F_DOWN), BF16),
        ],
    )
    return pl.pallas_call(
        _experts_kernel,
        grid_spec=grid_spec,
        out_shape=jax.ShapeDtypeStruct((N_BLK * R_BLK, D_MODEL), F32),
        compiler_params=_cparams("arbitrary", "arbitrary"),
        name="moe_experts",
    )(eid, rows, valid, ueff, xs, w_up, w_up, b_up4, b_up4, w_down, b_down4)


def _combine_kernel(pos_hbm, y_hbm, w_ref, x1_ref, gate2_ref, gpost2_ref, o_ref,
                    pos_smem, gbuf, sem_pos, sem_rows, *, tile_off):
    i = pl.program_id(0) + tile_off
    cp = pltpu.make_async_copy(pos_hbm.at[pl.ds(i, 1)], pos_smem, sem_pos)
    cp.start()
    cp.wait()

    def row_copy(t, k):
        src = pos_smem[0, k * TOK_TILE + t]
        return pltpu.make_async_copy(y_hbm.at[pl.ds(src, 1)], gbuf.at[k, pl.ds(t, 1)], sem_rows)

    def issue(t, c):
        for k in range(TOP_K):
            row_copy(t, k).start()
        return c

    lax.fori_loop(0, TOK_TILE, issue, 0)

    def drain(t, c):
        for k in range(TOP_K):
            row_copy(t, k).wait()
        return c

    lax.fori_loop(0, TOK_TILE, drain, 0)

    w = w_ref[...]
    m = w[:, 0:1] * gbuf[0]
    for k in range(1, TOP_K):
        m = m + w[:, k:k + 1] * gbuf[k]
    o_ref[...] = x1_ref[...] + gate2_ref[0] * (_rms(m) * gpost2_ref[...])


def _combine(pos_tiles, y_sorted, wgt_t, x1, mod3, g_post2, rows_per_mod, tile_off):
    m = x1.shape[0]
    r = mod3.shape[1]
    mod_row = lambda i: i // (rows_per_mod // TOK_TILE)
    return pl.pallas_call(
        functools.partial(_combine_kernel, tile_off=tile_off),
        grid=(m // TOK_TILE,),
        in_specs=[
            pl.BlockSpec(memory_space=pl.ANY),
            pl.BlockSpec(memory_space=pl.ANY),
            pl.BlockSpec((TOK_TILE, TOP_K), lambda i: (i, 0)),
            pl.BlockSpec((TOK_TILE, D_MODEL), lambda i: (i, 0)),
            pl.BlockSpec((1, r, D_MODEL), lambda i: (mod_row(i), 0, 5)),
            pl.BlockSpec((1, D_MODEL), lambda i: (0, 0)),
        ],
        out_specs=pl.BlockSpec((TOK_TILE, D_MODEL), lambda i: (i, 0)),
        out_shape=jax.ShapeDtypeStruct((m, D_MODEL), F32),
        scratch_shapes=[
            pltpu.SMEM((1, TOP_K * TOK_TILE), I32),
            pltpu.VMEM((TOP_K, TOK_TILE, D_MODEL), F32),
            pltpu.SemaphoreType.DMA(()),
            pltpu.SemaphoreType.DMA(()),
        ],
        compiler_params=_cparams("arbitrary"),
        name="moe_combine",
    )(pos_tiles, y_sorted, wgt_t, x1, mod3, g_post2)


def _route_tables(counts, idx_all, rank_all):
    nblk = (counts + (R_BLK - 1)) // R_BLK
    cum = jnp.cumsum(nblk)
    base = cum - nblk
    total = cum[-1]
    u = jnp.arange(N_BLK, dtype=I32)
    valid = u < total
    ueff = jnp.minimum(u, total - 1)
    eid = jnp.minimum(jnp.searchsorted(cum, ueff, side="right"), N_EXPERTS - 1).astype(I32)
    rows = jnp.where(valid, jnp.clip(counts[eid] - (ueff - base[eid]) * R_BLK, 0, R_BLK), 0)
    pos = base[idx_all] * R_BLK + rank_all
    pos_tiles = pos.reshape(TOP_K, N_TOK // TOK_TILE, TOK_TILE).transpose(1, 0, 2)
    pos_tiles = pos_tiles.reshape(N_TOK // TOK_TILE, TOP_K * TOK_TILE)
    return (eid, rows.astype(I32), valid.astype(I32), ueff.astype(I32)), pos_tiles.astype(I32)


def _layer_params(l, w):
    row = lambda a: a[l][None, :]
    perm = lambda a: jnp.concatenate([a[..., N_REST:], a[..., W_A:N_REST], a[..., :W_A]], axis=-1)
    return dict(
        g_pre1=row(w["g_pre1"]), g_post1=row(w["g_post1"]),
        g_pre2=row(w["g_pre2"]), g_post2=row(w["g_post2"]),
        w_in_b=perm(w["w_in"][l]).astype(BF16), b_in=perm(w["b_in"][l])[None, :],
        w_pool_b=w["w_pool"][l].astype(BF16), b_pool=row(w["b_pool"]), pool_scale=row(w["pool_scale"]),
        ln_v_g=row(w["ln_v_g"]), ln_v_b=row(w["ln_v_b"]),
        w_s=w["w_s"][l], bs_full=jnp.repeat(w["b_s"][l].T, CHUNK, axis=1),
        ws0=jnp.repeat(w["w_s"][l][:, 0, 0], CHUNK)[None, :],
        bs0=jnp.repeat(w["b_s"][l][:, 0], CHUNK)[None, :],
        w_dw=w["w_dw"][l], b_dw=row(w["b_dw"]), ln_c_g=row(w["ln_c_g"]), ln_c_b=row(w["ln_c_b"]),
        w_pa_b=w["w_pa"][l].astype(BF16), w_pb_b=w["w_pb"][l].astype(BF16),
        w_pc_b=w["w_pc"][l].astype(BF16), w_o_b=w["w_o"][l].astype(BF16),
        w_rt_b=w["w_router"][l].T.astype(BF16), b_rt=w["b_router"][l][:, None],
    )


def kernel(x_prompt, x_sample, c_prompt, c_sample, state_pool, state_conv, w_ada, b_ada, g_pre1, g_post1, g_pre2, g_post2, w_in, b_in, w_pool, b_pool, pool_scale, ln_v_g, ln_v_b, w_s, b_s, w_dw, b_dw, ln_c_g, ln_c_b, w_pa, w_pb, w_pc, w_o, w_router, b_router, w_up, b_up, w_down, b_down):
    weights = dict(g_pre1=g_pre1, g_post1=g_post1, g_pre2=g_pre2, g_post2=g_post2, w_in=w_in,
                   b_in=b_in, w_pool=w_pool, b_pool=b_pool, pool_scale=pool_scale, ln_v_g=ln_v_g,
                   ln_v_b=ln_v_b, w_s=w_s, b_s=b_s, w_dw=w_dw, b_dw=b_dw, ln_c_g=ln_c_g,
                   ln_c_b=ln_c_b, w_pa=w_pa, w_pb=w_pb, w_pc=w_pc, w_o=w_o, w_router=w_router,
                   b_router=b_router)
    xp = x_prompt.reshape(N_PROMPT, D_MODEL)
    xs = x_sample.reshape(DEC_BATCH, D_MODEL)
    c_all = jnp.concatenate(
        [c_sample, c_prompt, jnp.zeros((N_MOD_PAD - N_MOD, D_MODEL), F32)], axis=0)
    b_ada3 = b_ada[:, None, :]
    b_up4 = b_up[:, :, None, :]
    b_down4 = b_down[:, :, None, :]
    zero_cnt = jnp.zeros((N_EXPERTS, 128), F32)

    pool_p, pool_s, conv_p, conv_s, v_s = [], [], [], [], []
    for l in range(DEPTH):
        p = _layer_params(l, weights)
        mod = _ada(c_all, w_ada, b_ada3, l)
        mod_s = mod[:DEC_BATCH][None]
        mod_p = mod[DEC_BATCH:N_MOD][:, None, :]

        gates_p, zr_p = _inproj(xp, mod_p, p["g_pre1"], p["w_in_b"], p["b_in"], TM_IN, SEQ)
        gates_s, zr_s = _inproj(xs, mod_s, p["g_pre1"], p["w_in_b"], p["b_in"], DEC_BATCH, DEC_BATCH)
        pa_p, sb_p, sc_p, npool, nconv = _mixer_prompt(zr_p, p)
        spool_t = jnp.transpose(state_pool[l], (1, 0, 2))
        sconv_t = jnp.transpose(state_conv[l], (1, 0, 2))
        pa_s, sb_s, sc_s, glu_s, vn_s = _mixer_sample(zr_s, spool_t, sconv_t, p)

        x1_p, h2_all, idx_p, wgt_p, rank_p, cnt_p = _merge(
            pa_p, sb_p, sc_p, gates_p, xp, mod_p, p, zero_cnt, TM_MERGE, SEQ, None, 0)
        x1_s, h2_all, idx_s, wgt_s, rank_s, cnt_s = _merge(
            pa_s, sb_s, sc_s, gates_s, xs, mod_s, p, cnt_p, DEC_BATCH, DEC_BATCH, h2_all,
            N_PROMPT // DEC_BATCH)

        counts = cnt_s[:, 0].astype(I32)
        idx_all = jnp.concatenate([idx_p, idx_s], axis=1)
        rank_all = jnp.concatenate([rank_p, rank_s], axis=1)
        meta, pos_tiles = _route_tables(counts, idx_all, rank_all)
        x_sorted = _dispatch(pos_tiles, h2_all)
        y_sorted = _experts(meta, x_sorted, w_up, b_up4, w_down, b_down4, l)
        xp = _combine(pos_tiles, y_sorted, wgt_p.T, x1_p, mod_p, p["g_post2"], SEQ, 0)
        xs = _combine(pos_tiles, y_sorted, wgt_s.T, x1_s, mod_s, p["g_post2"], DEC_BATCH,
                      N_PROMPT // TOK_TILE)

        pool_p.append(npool[:, POOL_HALO - POOL_BUF:])
        conv_p.append(nconv[:, CONV_HALO - CONV_BUF:])
        a_s = zr_s[:, 2 * W_B + 2 * W_C:]
        pool_s.append(jnp.concatenate([state_pool[l][:, 1:], a_s[:, None, :]], axis=1))
        conv_s.append(jnp.concatenate([state_conv[l][:, 1:], glu_s[:, None, :]], axis=1))
        v_s.append(vn_s[:, None, :])

    return (xp.reshape(BATCH, SEQ, D_MODEL), xs.reshape(DEC_BATCH, 1, D_MODEL),
            jnp.stack(pool_p), jnp.stack(pool_s), jnp.stack(conv_p), jnp.stack(conv_s),
            jnp.stack(v_s))
```

```python
import functools

import jax
import jax.numpy as jnp
from jax import lax
from jax.experimental import pallas as pl
from jax.experimental.pallas import tpu as pltpu

F32 = jnp.float32
BF16 = jnp.bfloat16
I32 = jnp.int32

D_MODEL = 2048
BATCH = 4
SEQ = 2048
DEPTH = 2
DEC_BATCH = 128
PAST_LEN = 16384
W_A = 512
POOL_WINDOWS = (2, 4, 8, 16)
POOL_GROUP = 128
POOL_BUF = 15
W_B = 768
CHUNK = 128
H_B = 6
W_C = 768
CONV_WIDTH = 31
CONV_BUF = 30
N_GATES = 3 * D_MODEL
N_REST = W_A + 2 * W_B + 2 * W_C
N_IN = N_REST + N_GATES
N_EXPERTS = 32
TOP_K = 4
D_FF = D_MODEL
SWIGLU_LIMIT = 7.0
SWIGLU_ALPHA = 1.702
EPS = 1e-6

N_PROMPT = BATCH * SEQ
N_TOK = N_PROMPT + DEC_BATCH
N_MOD = DEC_BATCH + BATCH
N_MOD_PAD = 136

LANES = 128
SUBLANES = 8

VMEM_LIMIT = 56 * 1024 * 1024

TN_ADA = 1024
TM_IN = 1024
TN_IN = 512
TT_MIX = 256
POOL_HALO = 16
CONV_HALO = 32
TM_MERGE = 256
TN_MERGE_S = 512
TOK_TILE = 128
R_BLK = 1536
R_UNIT = 128
PIECE = 512
F_UP = 256
F_DOWN = 512
NC_UP = D_FF // F_UP
NC_DOWN = D_MODEL // F_DOWN
N_BLK = N_EXPERTS + (N_TOK * TOP_K) // R_BLK

HI = lax.Precision.HIGHEST


def _cparams(*sem):
    return pltpu.CompilerParams(dimension_semantics=sem, vmem_limit_bytes=VMEM_LIMIT)


def _sigmoid(x):
    return 1.0 / (1.0 + jnp.exp(-x))


def _gelu_tanh(x):
    c = 0.7978845608028654
    return x * (0.5 * (1.0 + jnp.tanh(c * (x + 0.044715 * (x * x * x)))))


def _rms(x):
    return x * lax.rsqrt(jnp.mean(x * x, axis=-1, keepdims=True) + EPS)


def _layernorm(x, g, b):
    mu = jnp.mean(x, axis=-1, keepdims=True)
    xc = x - mu
    return xc * lax.rsqrt(jnp.mean(xc * xc, axis=-1, keepdims=True) + EPS) * g + b


def _dot(a, b):
    return jnp.dot(a, b, preferred_element_type=F32)


def _dot32(a, b):
    return jnp.dot(a, b, preferred_element_type=F32, precision=HI)


def _ada_kernel(c_ref, w_ref, b_ref, o_ref):
    c = c_ref[...]
    s = (c * _sigmoid(c)).astype(BF16)
    o_ref[...] = _dot(s, w_ref[0].astype(BF16)) + b_ref[0]


def _ada(c_all, w_ada, b_ada3, layer):
    n = 6 * D_MODEL
    return pl.pallas_call(
        _ada_kernel,
        grid=(n // TN_ADA,),
        in_specs=[
            pl.BlockSpec((N_MOD_PAD, D_MODEL), lambda j: (0, 0)),
            pl.BlockSpec((1, D_MODEL, TN_ADA), lambda j: (layer, 0, j)),
            pl.BlockSpec((1, 1, TN_ADA), lambda j: (layer, 0, j)),
        ],
        out_specs=pl.BlockSpec((N_MOD_PAD, TN_ADA), lambda j: (0, j)),
        out_shape=jax.ShapeDtypeStruct((N_MOD_PAD, n), F32),
        compiler_params=_cparams("arbitrary"),
        name="ada",
    )(c_all, w_ada, b_ada3)


N_GATE_TILES = N_GATES // TN_IN
N_REST_TILES = N_REST // TN_IN


def _in_col_tile(j):
    a_tiles = W_A // TN_IN
    return jnp.where(j < N_GATE_TILES, j + N_REST_TILES,
                     jnp.where(j < N_GATE_TILES + N_REST_TILES - a_tiles,
                               j - N_GATE_TILES + a_tiles, j - (N_GATE_TILES + N_REST_TILES - a_tiles)))


def _inproj_kernel(x_ref, sc_ref, sh_ref, g_ref, w_ref, b_ref, gates_ref, z_ref, h_scr):
    j = pl.program_id(1)

    @pl.when(j == 0)
    def _():
        y = _rms(x_ref[...]) * g_ref[...]
        h_scr[...] = (y * (1.0 + sc_ref[0]) + sh_ref[0]).astype(BF16)

    z = _dot(h_scr[...], w_ref[...]) + b_ref[0]

    @pl.when(j < N_GATE_TILES)
    def _():
        gates_ref[...] = _sigmoid(z).astype(BF16)

    @pl.when(j >= N_GATE_TILES)
    def _():
        z_ref[...] = z


def _inproj(x, mod3, g_pre, w_in_b, b_in3, layer):
    m = x.shape[0]
    mod_row = lambda i: i // (SEQ // TM_IN)
    return pl.pallas_call(
        _inproj_kernel,
        grid=(m // TM_IN, N_IN // TN_IN),
        in_specs=[
            pl.BlockSpec((TM_IN, D_MODEL), lambda i, j: (i, 0)),
            pl.BlockSpec((1, 1, D_MODEL), lambda i, j: (mod_row(i), 0, 1)),
            pl.BlockSpec((1, 1, D_MODEL), lambda i, j: (mod_row(i), 0, 0)),
            pl.BlockSpec((1, D_MODEL), lambda i, j: (0, 0)),
            pl.BlockSpec((D_MODEL, TN_IN), lambda i, j: (0, _in_col_tile(j))),
            pl.BlockSpec((1, 1, TN_IN), lambda i, j: (layer, 0, _in_col_tile(j))),
        ],
        out_specs=[
            pl.BlockSpec((TM_IN, TN_IN), lambda i, j: (i, jnp.minimum(j, N_GATE_TILES - 1))),
            pl.BlockSpec((TM_IN, TN_IN), lambda i, j: (i, jnp.maximum(j - N_GATE_TILES, 0))),
        ],
        out_shape=[
            jax.ShapeDtypeStruct((m, N_GATES), BF16),
            jax.ShapeDtypeStruct((m, N_REST), F32),
        ],
        scratch_shapes=[pltpu.VMEM((TM_IN, D_MODEL), BF16)],
        compiler_params=_cparams("arbitrary", "arbitrary"),
        name="inproj",
    )(x, mod3, mod3, g_pre, w_in_b, b_in3)


def _inproj_sample_kernel(x_ref, sc_ref, sh_ref, g_ref, w_ref, b_ref, z_ref, h_scr):
    @pl.when(pl.program_id(0) == 0)
    def _():
        y = _rms(x_ref[...]) * g_ref[...]
        h_scr[...] = y * (1.0 + sc_ref[0]) + sh_ref[0]

    z_ref[...] = _dot32(h_scr[...], w_ref[0]) + b_ref[0]


def _inproj_sample(x, mod3, g_pre, w_in, b_in3, layer):
    n = DEC_BATCH
    return pl.pallas_call(
        _inproj_sample_kernel,
        grid=(N_IN // TN_IN,),
        in_specs=[
            pl.BlockSpec((n, D_MODEL), lambda j: (0, 0)),
            pl.BlockSpec((1, n, D_MODEL), lambda j: (0, 0, 1)),
            pl.BlockSpec((1, n, D_MODEL), lambda j: (0, 0, 0)),
            pl.BlockSpec((1, D_MODEL), lambda j: (0, 0)),
            pl.BlockSpec((1, D_MODEL, TN_IN), lambda j: (layer, 0, j)),
            pl.BlockSpec((1, 1, TN_IN), lambda j: (layer, 0, j)),
        ],
        out_specs=pl.BlockSpec((n, TN_IN), lambda j: (0, j)),
        out_shape=jax.ShapeDtypeStruct((n, N_IN), F32),
        scratch_shapes=[pltpu.VMEM((n, D_MODEL), F32)],
        compiler_params=_cparams("arbitrary"),
        name="inproj_sample",
    )(x, mod3, mod3, g_pre, w_in, b_in3)


def _mixer_prompt_kernel(uv_ref, cg_ref, a_ref, wpool_ref, bpool_ref, pscale_ref,
                         lvg_ref, lvb_ref, ws_ref, bs_ref, wdw_ref, bdw_ref, lcg_ref, lcb_ref,
                         pa_ref, sb_ref, sc_ref, npool_ref, nconv_ref,
                         aext, gext, cv_scr):
    t = pl.program_id(1)
    tt = TT_MIX

    @pl.when(t == 0)
    def _():
        aext[0:POOL_HALO, :] = jnp.zeros((POOL_HALO, W_A), F32)
        gext[0:CONV_HALO, :] = jnp.zeros((CONV_HALO, W_C), F32)

    a = a_ref[...]
    aext[POOL_HALO:POOL_HALO + tt, :] = a
    pos = t * tt + lax.broadcasted_iota(I32, (tt, 1), 0)
    for g, w in enumerate(POOL_WINDOWS):
        c0, c1 = g * POOL_GROUP, (g + 1) * POOL_GROUP
        cur = a[:, c0:c1]
        s = cur
        for j in range(1, w):
            s = s + aext[POOL_HALO - j:POOL_HALO - j + tt, c0:c1]
        cnt = jnp.minimum(w, pos + 1).astype(F32)
        d = (s / cnt - cur).astype(BF16)
        og = _dot(d, wpool_ref[g])
        pa_ref[:, c0:c1] = ((og + bpool_ref[:, c0:c1]) * pscale_ref[:, c0:c1]).astype(BF16)
    tail_a = aext[tt:tt + POOL_HALO, :]
    npool_ref[0] = tail_a
    aext[0:POOL_HALO, :] = tail_a

    uv = _gelu_tanh(uv_ref[...])
    u = uv[:, :W_B]
    vn = _layernorm(uv[:, W_B:], lvg_ref[...], lvb_ref[...]).astype(BF16)
    ri = lax.broadcasted_iota(I32, (CHUNK, CHUNK), 0)
    ci = lax.broadcasted_iota(I32, (CHUNK, CHUNK), 1)
    for h in range(H_B):
        h0, h1 = h * CHUNK, (h + 1) * CHUNK
        wsm = jnp.where(ri >= ci, ws_ref[h], 0.0).astype(BF16)
        for c in range(tt // CHUNK):
            r0, r1 = c * CHUNK, (c + 1) * CHUNK
            s = _dot(wsm, vn[r0:r1, h0:h1]) + bs_ref[:, h0:h1]
            sb_ref[r0:r1, h0:h1] = (u[r0:r1, h0:h1] * s).astype(BF16)

    cg = cg_ref[...]
    gext[CONV_HALO:CONV_HALO + tt, :] = cg[:, :W_C] * _sigmoid(cg[:, W_C:])
    rb = 64
    for cb in range(W_C // LANES):
        c0, c1 = cb * LANES, (cb + 1) * LANES
        for r in range(tt // rb):
            base = CONV_HALO - CONV_BUF + r * rb
            acc = gext[base:base + rb, c0:c1] * wdw_ref[0:1, c0:c1]
            for k in range(1, CONV_WIDTH):
                acc = acc + gext[base + k:base + k + rb, c0:c1] * wdw_ref[k:k + 1, c0:c1]
            cv_scr[r * rb:(r + 1) * rb, c0:c1] = acc
    cv = _layernorm(cv_scr[...] + bdw_ref[...], lcg_ref[...], lcb_ref[...])
    sc_ref[...] = (cv * _sigmoid(cv)).astype(BF16)
    tail_g = gext[tt:tt + CONV_HALO, :]
    nconv_ref[0] = tail_g
    gext[0:CONV_HALO, :] = tail_g


def _mixer_prompt(zrest, p):
    nt = SEQ // TT_MIX
    row = lambda b, t: b * nt + t
    vec = lambda n: pl.BlockSpec((1, n), lambda b, t: (0, 0))
    return pl.pallas_call(
        _mixer_prompt_kernel,
        grid=(BATCH, nt),
        in_specs=[
            pl.BlockSpec((TT_MIX, 2 * W_B), lambda b, t: (row(b, t), 0)),
            pl.BlockSpec((TT_MIX, 2 * W_C), lambda b, t: (row(b, t), 1)),
            pl.BlockSpec((TT_MIX, W_A), lambda b, t: (row(b, t), (2 * W_B + 2 * W_C) // W_A)),
            pl.BlockSpec((len(POOL_WINDOWS), POOL_GROUP, POOL_GROUP), lambda b, t: (0, 0, 0)),
            vec(W_A), vec(W_A), vec(W_B), vec(W_B),
            pl.BlockSpec((H_B, CHUNK, CHUNK), lambda b, t: (0, 0, 0)),
            pl.BlockSpec((CHUNK, W_B), lambda b, t: (0, 0)),
            pl.BlockSpec((CONV_WIDTH, W_C), lambda b, t: (0, 0)),
            vec(W_C), vec(W_C), vec(W_C),
        ],
        out_specs=[
            pl.BlockSpec((TT_MIX, W_A), lambda b, t: (row(b, t), 0)),
            pl.BlockSpec((TT_MIX, W_B), lambda b, t: (row(b, t), 0)),
            pl.BlockSpec((TT_MIX, W_C), lambda b, t: (row(b, t), 0)),
            pl.BlockSpec((1, POOL_HALO, W_A), lambda b, t: (b, 0, 0)),
            pl.BlockSpec((1, CONV_HALO, W_C), lambda b, t: (b, 0, 0)),
        ],
        out_shape=[
            jax.ShapeDtypeStruct((N_PROMPT, W_A), BF16),
            jax.ShapeDtypeStruct((N_PROMPT, W_B), BF16),
            jax.ShapeDtypeStruct((N_PROMPT, W_C), BF16),
            jax.ShapeDtypeStruct((BATCH, POOL_HALO, W_A), F32),
            jax.ShapeDtypeStruct((BATCH, CONV_HALO, W_C), F32),
        ],
        scratch_shapes=[
            pltpu.VMEM((TT_MIX + POOL_HALO, W_A), F32),
            pltpu.VMEM((TT_MIX + CONV_HALO, W_C), F32),
            pltpu.VMEM((TT_MIX, W_C), F32),
        ],
        compiler_params=_cparams("arbitrary", "arbitrary"),
        name="mixer_prompt",
    )(zrest, zrest, zrest, p["w_pool_b"], p["b_pool"], p["pool_scale"], p["ln_v_g"], p["ln_v_b"],
      p["w_s"], p["bs_full"], p["w_dw"], p["b_dw"], p["ln_c_g"], p["ln_c_b"])


def _mixer_sample_kernel(z_ref, spool_ref, sconv_ref, wpool_ref, bpool_ref,
                         pscale_ref, lvg_ref, lvb_ref, ws0_ref, bs0_ref, wdw_ref, bdw_ref,
                         lcg_ref, lcb_ref, pa_ref, sb_ref, sc_ref, glu_ref, v_ref):
    a = z_ref[:, 0:W_A]
    for g, w in enumerate(POOL_WINDOWS):
        c0, c1 = g * POOL_GROUP, (g + 1) * POOL_GROUP
        cur = a[:, c0:c1]
        s = cur
        for j in range(1, w):
            s = s + spool_ref[POOL_BUF - j, :, c0:c1]
        cnt = float(min(w, PAST_LEN + 1))
        og = _dot32(s / cnt - cur, wpool_ref[0, g])
        pa_ref[:, c0:c1] = (og + bpool_ref[:, c0:c1]) * pscale_ref[:, c0:c1]

    u = _gelu_tanh(z_ref[:, W_A:W_A + W_B])
    vn = _layernorm(_gelu_tanh(z_ref[:, W_A + W_B:W_A + 2 * W_B]), lvg_ref[...], lvb_ref[...])
    v_ref[...] = vn
    sb_ref[...] = u * (ws0_ref[...] * vn + bs0_ref[...])

    c0 = W_A + 2 * W_B
    glu = z_ref[:, c0:c0 + W_C] * _sigmoid(z_ref[:, c0 + W_C:c0 + 2 * W_C])
    glu_ref[...] = glu
    acc = glu * wdw_ref[CONV_BUF:CONV_BUF + 1, :]
    for k in range(CONV_BUF):
        acc = acc + sconv_ref[k] * wdw_ref[k:k + 1, :]
    cv = _layernorm(acc + bdw_ref[...], lcg_ref[...], lcb_ref[...])
    sc_ref[...] = cv * _sigmoid(cv)


def _mixer_sample(z, spool_t, sconv_t, w_pool, p, layer):
    n = DEC_BATCH
    vec = lambda w: pl.BlockSpec((1, w), lambda i: (0, 0))
    out = lambda w: pl.BlockSpec((n, w), lambda i: (0, 0))
    return pl.pallas_call(
        _mixer_sample_kernel,
        grid=(1,),
        in_specs=[
            pl.BlockSpec((n, N_REST), lambda i: (0, 0)),
            pl.BlockSpec((POOL_BUF, n, W_A), lambda i: (0, 0, 0)),
            pl.BlockSpec((CONV_BUF, n, W_C), lambda i: (0, 0, 0)),
            pl.BlockSpec((1, len(POOL_WINDOWS), POOL_GROUP, POOL_GROUP), lambda i: (layer, 0, 0, 0)),
            vec(W_A), vec(W_A), vec(W_B), vec(W_B), vec(W_B), vec(W_B),
            pl.BlockSpec((CONV_WIDTH, W_C), lambda i: (0, 0)),
            vec(W_C), vec(W_C), vec(W_C),
        ],
        out_specs=[out(W_A), out(W_B), out(W_C), out(W_C), out(W_B)],
        out_shape=[
            jax.ShapeDtypeStruct((n, W_A), F32),
            jax.ShapeDtypeStruct((n, W_B), F32),
            jax.ShapeDtypeStruct((n, W_C), F32),
            jax.ShapeDtypeStruct((n, W_C), F32),
            jax.ShapeDtypeStruct((n, W_B), F32),
        ],
        compiler_params=_cparams("arbitrary"),
        name="mixer_sample",
    )(z, spool_t, sconv_t, w_pool, p["b_pool"], p["pool_scale"],
      p["ln_v_g"], p["ln_v_b"], p["ws0"], p["bs0"], p["w_dw"], p["b_dw"], p["ln_c_g"], p["ln_c_b"])


def _route(logits, idx_ref, wgt_ref, rank_ref, cnt_ref, cnt_scr):
    tm = logits.shape[1]
    eidx = lax.broadcasted_iota(I32, (N_EXPERTS, tm), 0).astype(F32)
    work = logits
    vals, onehots = [], []
    for k in range(TOP_K):
        mx = jnp.max(work, axis=0, keepdims=True)
        ik = jnp.min(jnp.where(work == mx, eidx, float(N_EXPERTS)), axis=0, keepdims=True)
        oh = eidx == ik
        vals.append(mx)
        onehots.append(oh)
        idx_ref[k:k + 1, :] = ik.astype(I32)
        work = jnp.where(oh, -jnp.inf, work)
    ex = [jnp.exp(v - vals[0]) for v in vals]
    den = ex[0] + ex[1] + ex[2] + ex[3]
    for k in range(TOP_K):
        wgt_ref[k:k + 1, :] = ex[k] / den

    sel = jnp.zeros((N_EXPERTS, tm), F32)
    for oh in onehots:
        sel = sel + jnp.where(oh, 1.0, 0.0)
    ri = lax.broadcasted_iota(I32, (tm, tm), 0)
    ci = lax.broadcasted_iota(I32, (tm, tm), 1)
    before = jnp.where(ri < ci, 1.0, 0.0).astype(BF16)
    base = cnt_scr[:, 0:1] + _dot(sel.astype(BF16), before)
    for k in range(TOP_K):
        rank_ref[k:k + 1, :] = jnp.sum(jnp.where(onehots[k], base, 0.0), axis=0,
                                       keepdims=True).astype(I32)
    cnt_scr[...] = cnt_scr[...] + jnp.sum(sel, axis=1, keepdims=True)
    cnt_ref[...] = cnt_scr[...]


def _pack_rows(h2, hp_ref):
    hp_ref[...] = h2


def _merge_kernel(pa_ref, sb_ref, sc_ref, gates_ref, x_ref, gate1_ref, scale2_ref, shift2_ref,
                  gpost1_ref, gpre2_ref, wpa_ref, wpb_ref, wpc_ref, wo_ref, wrt_ref, brt_ref,
                  cnt_in_ref, x1_ref, hp_ref, idx_ref, wgt_ref, rank_ref, cnt_ref, cnt_scr):
    @pl.when(pl.program_id(0) == 0)
    def _():
        cnt_scr[...] = cnt_in_ref[...]

    g = gates_ref[...]
    m = g[:, :D_MODEL].astype(F32) * _dot(pa_ref[...], wpa_ref[...])
    m = m + g[:, D_MODEL:2 * D_MODEL].astype(F32) * _dot(sb_ref[...], wpb_ref[...])
    m = m + g[:, 2 * D_MODEL:].astype(F32) * _dot(sc_ref[...], wpc_ref[...])
    o = _dot(m.astype(BF16), wo_ref[...])
    x1 = x_ref[...] + gate1_ref[0] * (_rms(o) * gpost1_ref[...])
    x1_ref[...] = x1
    h2 = _rms(x1) * gpre2_ref[...] * (1.0 + scale2_ref[0]) + shift2_ref[0]
    _pack_rows(h2, hp_ref)
    logits = lax.dot_general(wrt_ref[...], h2.astype(BF16), (((1,), (1,)), ((), ())),
                             preferred_element_type=F32) + brt_ref[...]
    _route(logits, idx_ref, wgt_ref, rank_ref, cnt_ref, cnt_scr)


def _merge(pa, sb, sc, gates, x, mod3, p, cnt_in):
    m = x.shape[0]
    tm = TM_MERGE
    mod_row = lambda i: i // (SEQ // tm)
    modspec = lambda k: pl.BlockSpec((1, 1, D_MODEL), lambda i: (mod_row(i), 0, k))
    const = lambda shape: pl.BlockSpec(shape, lambda i: (0,) * len(shape),
                                       pipeline_mode=pl.Buffered(1))
    row = lambda w: pl.BlockSpec((tm, w), lambda i: (i, 0))
    tok = pl.BlockSpec((TOP_K, tm), lambda i: (0, i))
    return pl.pallas_call(
        _merge_kernel,
        grid=(m // tm,),
        in_specs=[
            row(W_A), row(W_B), row(W_C), row(N_GATES), row(D_MODEL),
            modspec(2), modspec(4), modspec(3),
            const((1, D_MODEL)), const((1, D_MODEL)),
            const((W_A, D_MODEL)), const((W_B, D_MODEL)), const((W_C, D_MODEL)),
            const((D_MODEL, D_MODEL)), const((N_EXPERTS, D_MODEL)), const((N_EXPERTS, 1)),
            const((N_EXPERTS, LANES)),
        ],
        out_specs=[
            row(D_MODEL),
            row(D_MODEL),
            tok, tok, tok,
            pl.BlockSpec((N_EXPERTS, LANES), lambda i: (0, 0)),
        ],
        out_shape=[
            jax.ShapeDtypeStruct((m, D_MODEL), F32),
            jax.ShapeDtypeStruct((m, D_MODEL), F32),
            jax.ShapeDtypeStruct((TOP_K, m), I32),
            jax.ShapeDtypeStruct((TOP_K, m), F32),
            jax.ShapeDtypeStruct((TOP_K, m), I32),
            jax.ShapeDtypeStruct((N_EXPERTS, LANES), F32),
        ],
        scratch_shapes=[pltpu.VMEM((N_EXPERTS, LANES), F32)],
        compiler_params=_cparams("arbitrary"),
        name="merge_router",
    )(pa, sb, sc, gates, x, mod3, mod3, mod3, p["g_post1"], p["g_pre2"],
      p["w_pa_b"], p["w_pb_b"], p["w_pc_b"], p["w_o_b"], p["w_rt_b"], p["b_rt"], cnt_in)


NT_MERGE_S = D_MODEL // TN_MERGE_S


def _merge_sample_kernel(pa_ref, sb_ref, sc_ref, ga_ref, gb_ref, gc_ref, x_ref,
                         gate1_ref, scale2_ref, shift2_ref, gpost1_ref, gpre2_ref,
                         wpa_ref, wpb_ref, wpc_ref, wo_ref, wrt_ref, brt_ref, cnt_in_ref,
                         x1_ref, hp_ref, idx_ref, wgt_ref, rank_ref, cnt_ref,
                         m_scr, o_scr, cnt_scr):
    s = pl.program_id(0)
    tn = TN_MERGE_S

    @pl.when(s < NT_MERGE_S)
    def _():
        m = _sigmoid(ga_ref[...]) * _dot32(pa_ref[...], wpa_ref[0])
        m = m + _sigmoid(gb_ref[...]) * _dot32(sb_ref[...], wpb_ref[0])
        m = m + _sigmoid(gc_ref[...]) * _dot32(sc_ref[...], wpc_ref[0])
        m_scr[jnp.minimum(s, NT_MERGE_S - 1)] = m

    @pl.when(jnp.logical_and(s >= NT_MERGE_S, s < 2 * NT_MERGE_S))
    def _():
        o = _dot32(m_scr[0], wo_ref[0, 0:tn, :])
        for k in range(1, NT_MERGE_S):
            o = o + _dot32(m_scr[k], wo_ref[0, k * tn:(k + 1) * tn, :])
        o_scr[jnp.clip(s - NT_MERGE_S, 0, NT_MERGE_S - 1)] = o

    @pl.when(s == 2 * NT_MERGE_S)
    def _():
        cnt_scr[...] = cnt_in_ref[...]
        o = jnp.concatenate([o_scr[k] for k in range(NT_MERGE_S)], axis=1)
        x1 = x_ref[...] + gate1_ref[0] * (_rms(o) * gpost1_ref[...])
        x1_ref[...] = x1
        h2 = _rms(x1) * gpre2_ref[...] * (1.0 + scale2_ref[0]) + shift2_ref[0]
        _pack_rows(h2, hp_ref)
        logits = lax.dot_general(wrt_ref[...], h2, (((1,), (1,)), ((), ())),
                                 preferred_element_type=F32, precision=HI) + brt_ref[...]
        _route(logits, idx_ref, wgt_ref, rank_ref, cnt_ref, cnt_scr)


def _merge_sample(pa, sb, sc, z, x, mod3, p, w_pa, w_pb, w_pc, w_o, cnt_in, layer):
    n = DEC_BATCH
    tn = TN_MERGE_S
    nt = NT_MERGE_S
    t1 = lambda s: jnp.minimum(s, nt - 1)
    t2 = lambda s: jnp.clip(s - nt, 0, nt - 1)
    full = lambda shape: pl.BlockSpec(shape, lambda s: (0,) * len(shape))
    modspec = lambda k: pl.BlockSpec((1, n, D_MODEL), lambda s: (0, 0, k))
    gate_cols = lambda b: pl.BlockSpec((n, tn), lambda s: (0, (N_REST + b * D_MODEL) // tn + t1(s)))
    tok = pl.BlockSpec((TOP_K, n), lambda s: (0, 0))
    return pl.pallas_call(
        _merge_sample_kernel,
        grid=(2 * nt + 1,),
        in_specs=[
            full((n, W_A)), full((n, W_B)), full((n, W_C)),
            gate_cols(0), gate_cols(1), gate_cols(2),
            full((n, D_MODEL)),
            modspec(2), modspec(4), modspec(3),
            full((1, D_MODEL)), full((1, D_MODEL)),
            pl.BlockSpec((1, W_A, tn), lambda s: (layer, 0, t1(s))),
            pl.BlockSpec((1, W_B, tn), lambda s: (layer, 0, t1(s))),
            pl.BlockSpec((1, W_C, tn), lambda s: (layer, 0, t1(s))),
            pl.BlockSpec((1, D_MODEL, tn), lambda s: (layer, 0, t2(s))),
            full((N_EXPERTS, D_MODEL)), full((N_EXPERTS, 1)), full((N_EXPERTS, LANES)),
        ],
        out_specs=[
            full((n, D_MODEL)),
            full((n, D_MODEL)),
            tok, tok, tok,
            full((N_EXPERTS, LANES)),
        ],
        out_shape=[
            jax.ShapeDtypeStruct((n, D_MODEL), F32),
            jax.ShapeDtypeStruct((n, D_MODEL), F32),
            jax.ShapeDtypeStruct((TOP_K, n), I32),
            jax.ShapeDtypeStruct((TOP_K, n), F32),
            jax.ShapeDtypeStruct((TOP_K, n), I32),
            jax.ShapeDtypeStruct((N_EXPERTS, LANES), F32),
        ],
        scratch_shapes=[
            pltpu.VMEM((nt, n, tn), F32),
            pltpu.VMEM((nt, n, tn), F32),
            pltpu.VMEM((N_EXPERTS, LANES), F32),
        ],
        compiler_params=_cparams("arbitrary"),
        name="merge_router_sample",
    )(pa, sb, sc, z, z, z, x, mod3, mod3, mod3, p["g_post1"], p["g_pre2"],
      w_pa, w_pb, w_pc, w_o, p["w_rt"], p["b_rt"], cnt_in)


def _dispatch_kernel(pos_hbm, hp_ref, hs_ref, xs_hbm, pos_smem, sem_pos, sem_rows):
    i = pl.program_id(0)
    last = pl.num_programs(0) - 1
    cp = pltpu.make_async_copy(pos_hbm.at[pl.ds(i, 1)], pos_smem, sem_pos)
    cp.start()
    cp.wait()

    def scatter_rows(src_ref):
        def body(q, c):
            for dt in range(4):
                t = q * 4 + dt
                for k in range(TOP_K):
                    dst = pos_smem[0, k * TOK_TILE + t]
                    pltpu.make_async_copy(src_ref.at[pl.ds(t, 1)], xs_hbm.at[pl.ds(dst, 1)],
                                          sem_rows).start(priority=k % 2)
            return c
        lax.fori_loop(0, TOK_TILE // 4, body, 0)
        for _ in range(TOP_K):
            pltpu.make_async_copy(src_ref, xs_hbm.at[pl.ds(0, TOK_TILE)], sem_rows).wait()

    @pl.when(i < last)
    def _():
        scatter_rows(hp_ref)

    @pl.when(i == last)
    def _():
        scatter_rows(hs_ref)


def _dispatch(pos_tiles, hp_p, hp_s):
    n_prompt_tiles = N_PROMPT // TOK_TILE
    return pl.pallas_call(
        _dispatch_kernel,
        grid=(N_TOK // TOK_TILE,),
        in_specs=[
            pl.BlockSpec(memory_space=pl.ANY),
            pl.BlockSpec((TOK_TILE, D_MODEL), lambda i: (jnp.minimum(i, n_prompt_tiles - 1), 0)),
            pl.BlockSpec((TOK_TILE, D_MODEL), lambda i: (0, 0)),
        ],
        out_specs=pl.BlockSpec(memory_space=pl.ANY),
        out_shape=jax.ShapeDtypeStruct((N_BLK * R_BLK, D_MODEL), F32),
        scratch_shapes=[
            pltpu.SMEM((1, TOP_K * TOK_TILE), I32),
            pltpu.SemaphoreType.DMA(()),
            pltpu.SemaphoreType.DMA(()),
        ],
        compiler_params=_cparams("arbitrary"),
        name="moe_dispatch",
    )(pos_tiles, hp_p, hp_s)


def _experts_kernel(eid_ref, rows_ref, valid_ref, ueff_ref,
                    x_hbm, wg_ref, wu_ref, bg_ref, bu_ref, wd_ref, bd_ref, y_ref,
                    xstage, xbuf, actbuf, wgu, wdb, sem_x):
    del eid_ref, valid_ref, ueff_ref
    u = pl.program_id(0)
    s = pl.program_id(1)
    nrows = rows_ref[u]
    nunits = (nrows + (R_UNIT - 1)) // R_UNIT
    live = nrows > 0

    def x_copy(blk):
        return pltpu.make_async_copy(x_hbm.at[pl.ds(blk * R_BLK, R_BLK)], xstage, sem_x)

    def for_pieces(fn):
        upp = PIECE // R_UNIT
        nfull = nunits // upp
        rem = nunits - nfull * upp

        def body(r, c):
            fn(pl.multiple_of(r * PIECE, PIECE), PIECE)
            return c
        lax.fori_loop(0, nfull, body, 0)
        off = nfull * PIECE

        @pl.when(rem >= 2)
        def _():
            fn(pl.multiple_of(off, 2 * R_UNIT), 2 * R_UNIT)

        @pl.when(rem % 2 == 1)
        def _():
            fn(pl.multiple_of(off + jnp.where(rem >= 2, 2 * R_UNIT, 0), R_UNIT), R_UNIT)

    @pl.when(jnp.logical_and(live, jnp.logical_and(u == 0, s == 0)))
    def _():
        x_copy(0).start()

    @pl.when(jnp.logical_and(live, s == 0))
    def _():
        x_copy(u).wait()

        def stage(r, c):
            r0 = pl.multiple_of(r * R_UNIT, R_UNIT)
            keep = (r0 + lax.broadcasted_iota(I32, (R_UNIT, 1), 0)) < nrows
            xbuf[pl.ds(r0, R_UNIT), :] = jnp.where(keep, xstage[pl.ds(r0, R_UNIT), :], 0.0).astype(BF16)
            return c
        lax.fori_loop(0, nunits, stage, 0)

    nxt = jnp.minimum(u + 1, N_BLK - 1)

    @pl.when(jnp.logical_and(s == 1, jnp.logical_and(u + 1 < N_BLK, rows_ref[nxt] > 0)))
    def _():
        x_copy(nxt).start()

    @pl.when(jnp.logical_and(live, s < NC_UP))
    def _():
        wgu[:, 0:F_UP] = wg_ref[0, 0].astype(BF16)
        wgu[:, F_UP:2 * F_UP] = wu_ref[0, 0].astype(BF16)
        chunk = jnp.minimum(s, NC_UP - 1)

        def piece(r0, size):
            gu = _dot(xbuf[pl.ds(r0, size), :], wgu[...])
            gate = jnp.minimum(gu[:, :F_UP] + bg_ref[0, 0], SWIGLU_LIMIT)
            up = jnp.clip(gu[:, F_UP:] + bu_ref[0, 0], -SWIGLU_LIMIT, SWIGLU_LIMIT)
            act = gate * _sigmoid(SWIGLU_ALPHA * gate) * (up + 1.0)
            actbuf[chunk, pl.ds(r0, size), :] = act.astype(BF16)
        for_pieces(piece)

    @pl.when(jnp.logical_and(live, s >= NC_UP))
    def _():
        wdb[...] = wd_ref[0, 0].astype(BF16)

        def piece(r0, size):
            acc = _dot(actbuf[0, pl.ds(r0, size), :], wdb[0:F_UP, :])
            for k in range(1, NC_UP):
                acc = acc + _dot(actbuf[k, pl.ds(r0, size), :], wdb[k * F_UP:(k + 1) * F_UP, :])
            y_ref[pl.ds(r0, size), :] = acc + bd_ref[0, 0]
        for_pieces(piece)

        def zero(r, c):
            y_ref[pl.ds(pl.multiple_of(r * R_UNIT, R_UNIT), R_UNIT), :] = jnp.zeros((R_UNIT, F_DOWN), F32)
            return c
        lax.fori_loop(nunits, R_BLK // R_UNIT, zero, 0)


def _experts(meta, xs, w_up, b_up4, w_down, b_down4, layer):
    eid, rows, valid, ueff = meta

    def up_chunk(u, s, valid):
        return jnp.where(valid[u] == 1, jnp.minimum(s, NC_UP - 1), NC_UP - 1)

    def down_chunk(u, s, valid):
        return jnp.where(valid[u] == 1, jnp.maximum(s - NC_UP, 0), NC_DOWN - 1)

    grid_spec = pltpu.PrefetchScalarGridSpec(
        num_scalar_prefetch=4,
        grid=(N_BLK, NC_UP + NC_DOWN),
        in_specs=[
            pl.BlockSpec(memory_space=pl.ANY),
            pl.BlockSpec((1, 1, D_MODEL, F_UP),
                         lambda u, s, e, r, v, f: (layer, e[u], 0, up_chunk(u, s, v))),
            pl.BlockSpec((1, 1, D_MODEL, F_UP),
                         lambda u, s, e, r, v, f: (layer, e[u], 0, NC_UP + up_chunk(u, s, v))),
            pl.BlockSpec((1, 1, 1, F_UP),
                         lambda u, s, e, r, v, f: (layer, e[u], 0, up_chunk(u, s, v))),
            pl.BlockSpec((1, 1, 1, F_UP),
                         lambda u, s, e, r, v, f: (layer, e[u], 0, NC_UP + up_chunk(u, s, v))),
            pl.BlockSpec((1, 1, D_FF, F_DOWN),
                         lambda u, s, e, r, v, f: (layer, e[u], 0, down_chunk(u, s, v))),
            pl.BlockSpec((1, 1, 1, F_DOWN),
                         lambda u, s, e, r, v, f: (layer, e[u], 0, down_chunk(u, s, v))),
        ],
        out_specs=pl.BlockSpec((R_BLK, F_DOWN), lambda u, s, e, r, v, f: (f[u], down_chunk(u, s, v))),
        scratch_shapes=[
            pltpu.VMEM((R_BLK, D_MODEL), F32),
            pltpu.VMEM((R_BLK, D_MODEL), BF16),
            pltpu.VMEM((NC_UP, R_BLK, F_UP), BF16),
            pltpu.VMEM((D_MODEL, 2 * F_UP), BF16),
            pltpu.VMEM((D_FF, F_DOWN), BF16),
            pltpu.SemaphoreType.DMA(()),
        ],
    )
    return pl.pallas_call(
        _experts_kernel,
        grid_spec=grid_spec,
        out_shape=jax.ShapeDtypeStruct((N_BLK * R_BLK, D_MODEL), F32),
        compiler_params=_cparams("arbitrary", "arbitrary"),
        name="moe_experts",
    )(eid, rows, valid, ueff, xs, w_up, w_up, b_up4, b_up4, w_down, b_down4)


def _combine_kernel(pos_hbm, y_hbm, w_ref, x1_ref, gate2_ref, gpost2_ref, o_ref,
                    pos_smem, gbuf, sem_pos, sem_rows, *, tile_off):
    i = pl.program_id(0)
    n = pl.num_programs(0)
    slot = i % 2

    def pos_copy(step, sl):
        return pltpu.make_async_copy(pos_hbm.at[pl.ds(step + tile_off, 1)], pos_smem.at[sl],
                                     sem_pos.at[sl])

    def issue_rows(sl):
        def body(q, c):
            for dt in range(4):
                t = q * 4 + dt
                for k in range(TOP_K):
                    src = pos_smem[sl, 0, k * TOK_TILE + t]
                    pltpu.make_async_copy(y_hbm.at[pl.ds(src, 1)], gbuf.at[sl, k, pl.ds(t, 1)],
                                          sem_rows.at[sl]).start(priority=k % 2)
            return c
        lax.fori_loop(0, TOK_TILE // 4, body, 0)

    @pl.when(i == 0)
    def _():
        pos_copy(0, 0).start()
        pos_copy(0, 0).wait()
        issue_rows(0)

        @pl.when(n > 1)
        def _():
            pos_copy(1, 1).start()

    @pl.when(i + 1 < n)
    def _():
        pos_copy(i + 1, 1 - slot).wait()
        issue_rows(1 - slot)

    @pl.when(i + 2 < n)
    def _():
        pos_copy(i + 2, slot).start()

    for k in range(TOP_K):
        pltpu.make_async_copy(y_hbm.at[pl.ds(0, TOK_TILE)], gbuf.at[slot, k], sem_rows.at[slot]).wait()

    w = w_ref[...]
    m = w[:, 0:1] * gbuf[slot, 0]
    for k in range(1, TOP_K):
        m = m + w[:, k:k + 1] * gbuf[slot, k]
    o_ref[...] = x1_ref[...] + gate2_ref[0] * (_rms(m) * gpost2_ref[...])


def _combine(pos_tiles, y_sorted, wgt_t, x1, mod3, g_post2, rows_per_mod, tile_off):
    m = x1.shape[0]
    r = mod3.shape[1]
    mod_row = lambda i: i // (rows_per_mod // TOK_TILE)
    return pl.pallas_call(
        functools.partial(_combine_kernel, tile_off=tile_off),
        grid=(m // TOK_TILE,),
        in_specs=[
            pl.BlockSpec(memory_space=pl.ANY),
            pl.BlockSpec(memory_space=pl.ANY),
            pl.BlockSpec((TOK_TILE, TOP_K), lambda i: (i, 0)),
            pl.BlockSpec((TOK_TILE, D_MODEL), lambda i: (i, 0)),
            pl.BlockSpec((1, r, D_MODEL), lambda i: (mod_row(i), 0, 5)),
            pl.BlockSpec((1, D_MODEL), lambda i: (0, 0)),
        ],
        out_specs=pl.BlockSpec((TOK_TILE, D_MODEL), lambda i: (i, 0)),
        out_shape=jax.ShapeDtypeStruct((m, D_MODEL), F32),
        scratch_shapes=[
            pltpu.SMEM((2, 1, TOP_K * TOK_TILE), I32),
            pltpu.VMEM((2, TOP_K, TOK_TILE, D_MODEL), F32),
            pltpu.SemaphoreType.DMA((2,)),
            pltpu.SemaphoreType.DMA((2,)),
        ],
        compiler_params=_cparams("arbitrary"),
        name="moe_combine",
    )(pos_tiles, y_sorted, wgt_t, x1, mod3, g_post2)


def _route_tables(counts, idx_all, rank_all):
    nblk = (counts + (R_BLK - 1)) // R_BLK
    cum = jnp.cumsum(nblk)
    base = cum - nblk
    total = cum[-1]
    u = jnp.arange(N_BLK, dtype=I32)
    valid = u < total
    ueff = jnp.minimum(u, jnp.maximum(total - 1, 0))
    eid = jnp.minimum(jnp.sum((cum[None, :] <= ueff[:, None]).astype(I32), axis=1), N_EXPERTS - 1)
    onehot_u = (eid[:, None] == jnp.arange(N_EXPERTS, dtype=I32)[None, :]).astype(I32)
    cnt_u = jnp.sum(onehot_u * counts[None, :], axis=1)
    base_u = jnp.sum(onehot_u * base[None, :], axis=1)
    rows = jnp.where(valid, jnp.clip(cnt_u - (ueff - base_u) * R_BLK, 0, R_BLK), 0)
    e_iota = jnp.arange(N_EXPERTS, dtype=I32)[None, None, :]
    base_tok = jnp.sum(jnp.where(idx_all[:, :, None] == e_iota, base[None, None, :], 0), axis=-1)
    pos = base_tok * R_BLK + rank_all
    pos_tiles = pos.reshape(TOP_K, N_TOK // TOK_TILE, TOK_TILE).transpose(1, 0, 2)
    pos_tiles = pos_tiles.reshape(N_TOK // TOK_TILE, TOP_K * TOK_TILE)
    return (eid.astype(I32), rows.astype(I32), valid.astype(I32), ueff.astype(I32)), pos_tiles.astype(I32)


def _layer_params(l, w):
    row = lambda a: a[l][None, :]
    return dict(
        g_pre1=row(w["g_pre1"]), g_post1=row(w["g_post1"]),
        g_pre2=row(w["g_pre2"]), g_post2=row(w["g_post2"]),
        w_in_b=w["w_in"][l].astype(BF16),
        w_pool_b=w["w_pool"][l].astype(BF16), b_pool=row(w["b_pool"]), pool_scale=row(w["pool_scale"]),
        ln_v_g=row(w["ln_v_g"]), ln_v_b=row(w["ln_v_b"]),
        w_s=w["w_s"][l], bs_full=jnp.repeat(w["b_s"][l].T, CHUNK, axis=1),
        ws0=jnp.repeat(w["w_s"][l][:, 0, 0], CHUNK)[None, :],
        bs0=jnp.repeat(w["b_s"][l][:, 0], CHUNK)[None, :],
        w_dw=w["w_dw"][l], b_dw=row(w["b_dw"]), ln_c_g=row(w["ln_c_g"]), ln_c_b=row(w["ln_c_b"]),
        w_pa_b=w["w_pa"][l].astype(BF16), w_pb_b=w["w_pb"][l].astype(BF16),
        w_pc_b=w["w_pc"][l].astype(BF16), w_o_b=w["w_o"][l].astype(BF16),
        w_rt=w["w_router"][l].T, w_rt_b=w["w_router"][l].T.astype(BF16),
        b_rt=w["b_router"][l][:, None],
    )


def kernel(x_prompt, x_sample, c_prompt, c_sample, state_pool, state_conv, w_ada, b_ada, g_pre1, g_post1, g_pre2, g_post2, w_in, b_in, w_pool, b_pool, pool_scale, ln_v_g, ln_v_b, w_s, b_s, w_dw, b_dw, ln_c_g, ln_c_b, w_pa, w_pb, w_pc, w_o, w_router, b_router, w_up, b_up, w_down, b_down):
    weights = dict(g_pre1=g_pre1, g_post1=g_post1, g_pre2=g_pre2, g_post2=g_post2, w_in=w_in,
                   w_pool=w_pool, b_pool=b_pool, pool_scale=pool_scale, ln_v_g=ln_v_g,
                   ln_v_b=ln_v_b, w_s=w_s, b_s=b_s, w_dw=w_dw, b_dw=b_dw, ln_c_g=ln_c_g,
                   ln_c_b=ln_c_b, w_pa=w_pa, w_pb=w_pb, w_pc=w_pc, w_o=w_o, w_router=w_router,
                   b_router=b_router)
    xp = x_prompt.reshape(N_PROMPT, D_MODEL)
    xs = x_sample.reshape(DEC_BATCH, D_MODEL)
    c_all = jnp.concatenate(
        [c_sample, c_prompt, jnp.zeros((N_MOD_PAD - N_MOD, D_MODEL), F32)], axis=0)
    b_ada3 = b_ada[:, None, :]
    b_in3 = b_in[:, None, :]
    b_up4 = b_up[:, :, None, :]
    b_down4 = b_down[:, :, None, :]
    zero_cnt = jnp.zeros((N_EXPERTS, LANES), F32)

    pool_p, pool_s, conv_p, conv_s, v_s = [], [], [], [], []
    for l in range(DEPTH):
        p = _layer_params(l, weights)
        mod = _ada(c_all, w_ada, b_ada3, l)
        mod_s = mod[:DEC_BATCH][None]
        mod_p = mod[DEC_BATCH:N_MOD][:, None, :]

        gates_p, zr_p = _inproj(xp, mod_p, p["g_pre1"], p["w_in_b"], b_in3, l)
        z_s = _inproj_sample(xs, mod_s, p["g_pre1"], w_in, b_in3, l)
        pa_p, sb_p, sc_p, npool, nconv = _mixer_prompt(zr_p, p)
        spool_t = jnp.transpose(state_pool[l], (1, 0, 2))
        sconv_t = jnp.transpose(state_conv[l], (1, 0, 2))
        pa_s, sb_s, sc_s, glu_s, vn_s = _mixer_sample(z_s, spool_t, sconv_t, w_pool, p, l)

        x1_p, hp_p, idx_p, wgt_p, rank_p, cnt_p = _merge(
            pa_p, sb_p, sc_p, gates_p, xp, mod_p, p, zero_cnt)
        x1_s, hp_s, idx_s, wgt_s, rank_s, cnt_s = _merge_sample(
            pa_s, sb_s, sc_s, z_s, xs, mod_s, p, w_pa, w_pb, w_pc, w_o, cnt_p, l)

        counts = cnt_s[:, 0].astype(I32)
        idx_all = jnp.concatenate([idx_p, idx_s], axis=1)
        rank_all = jnp.concatenate([rank_p, rank_s], axis=1)
        meta, pos_tiles = _route_tables(counts, idx_all, rank_all)
        x_sorted = _dispatch(pos_tiles, hp_p, hp_s)
        y_sorted = _experts(meta, x_sorted, w_up, b_up4, w_down, b_down4, l)
        xp = _combine(pos_tiles, y_sorted, wgt_p.T, x1_p, mod_p, p["g_post2"], SEQ, 0)
        xs = _combine(pos_tiles, y_sorted, wgt_s.T, x1_s, mod_s, p["g_post2"], DEC_BATCH,
                      N_PROMPT // TOK_TILE)

        pool_p.append(npool[:, POOL_HALO - POOL_BUF:])
        conv_p.append(nconv[:, CONV_HALO - CONV_BUF:])
        pool_s.append(jnp.concatenate([state_pool[l][:, 1:], z_s[:, None, :W_A]], axis=1))
        conv_s.append(jnp.concatenate([state_conv[l][:, 1:], glu_s[:, None, :]], axis=1))
        v_s.append(vn_s[:, None, :])

    return (xp.reshape(BATCH, SEQ, D_MODEL), xs.reshape(DEC_BATCH, 1, D_MODEL),
            jnp.stack(pool_p), jnp.stack(pool_s), jnp.stack(conv_p), jnp.stack(conv_s),
            jnp.stack(v_s))
```

```python
import functools

import jax
import jax.numpy as jnp
from jax import lax
from jax.experimental import pallas as pl
from jax.experimental.pallas import tpu as pltpu

F32 = jnp.float32
BF16 = jnp.bfloat16
I32 = jnp.int32

D_MODEL = 2048
BATCH = 4
SEQ = 2048
DEPTH = 2
DEC_BATCH = 128
PAST_LEN = 16384
W_A = 512
POOL_WINDOWS = (2, 4, 8, 16)
POOL_GROUP = 128
POOL_BUF = 15
W_B = 768
CHUNK = 128
H_B = 6
W_C = 768
CONV_WIDTH = 31
CONV_BUF = 30
N_GATES = 3 * D_MODEL
N_REST = W_A + 2 * W_B + 2 * W_C
N_IN = N_REST + N_GATES
N_EXPERTS = 32
TOP_K = 4
D_FF = D_MODEL
SWIGLU_LIMIT = 7.0
SWIGLU_ALPHA = 1.702
EPS = 1e-6

N_PROMPT = BATCH * SEQ
N_TOK = N_PROMPT + 2 * DEC_BATCH
N_MOD = DEC_BATCH + BATCH
N_MOD_PAD = 136

LANES = 128
SUBLANES = 8

VMEM_LIMIT = 56 * 1024 * 1024

TN_ADA = 1024
TM_IN = 1024
TN_IN = 512
TT_MIX = 256
POOL_HALO = 16
CONV_HALO = 32
TM_MERGE = 256
TN_MERGE_S = 512
TOK_TILE = 128
R_BLK = 1536
R_UNIT = 128
ROW_CLASSES = (896, 1152, R_BLK)
F_UP = 256
F_DOWN = 512
NC_UP = D_FF // F_UP
NC_DOWN = D_MODEL // F_DOWN
N_BLK = N_EXPERTS + (N_TOK * TOP_K) // R_BLK

HI = lax.Precision.HIGHEST


def _cparams(*sem):
    return pltpu.CompilerParams(dimension_semantics=sem, vmem_limit_bytes=VMEM_LIMIT)


def _sigmoid(x):
    return 1.0 / (1.0 + jnp.exp(-x))


def _gelu_tanh(x):
    c = 0.7978845608028654
    return x * (0.5 * (1.0 + jnp.tanh(c * (x + 0.044715 * (x * x * x)))))


def _rms(x):
    return x * lax.rsqrt(jnp.mean(x * x, axis=-1, keepdims=True) + EPS)


def _layernorm(x, g, b):
    mu = jnp.mean(x, axis=-1, keepdims=True)
    xc = x - mu
    return xc * lax.rsqrt(jnp.mean(xc * xc, axis=-1, keepdims=True) + EPS) * g + b


def _dot(a, b):
    return jnp.dot(a, b, preferred_element_type=F32)


def _dot32(a, b):
    return jnp.dot(a, b, preferred_element_type=F32, precision=HI)


def _ada_kernel(c_ref, w_ref, b_ref, o_ref):
    c = c_ref[...]
    s = (c * _sigmoid(c)).astype(BF16)
    o_ref[...] = _dot(s, w_ref[0].astype(BF16)) + b_ref[0]


def _ada(c_all, w_ada, b_ada3, layer):
    n = 6 * D_MODEL
    return pl.pallas_call(
        _ada_kernel,
        grid=(n // TN_ADA,),
        in_specs=[
            pl.BlockSpec((N_MOD_PAD, D_MODEL), lambda j: (0, 0)),
            pl.BlockSpec((1, D_MODEL, TN_ADA), lambda j: (layer, 0, j)),
            pl.BlockSpec((1, 1, TN_ADA), lambda j: (layer, 0, j)),
        ],
        out_specs=pl.BlockSpec((N_MOD_PAD, TN_ADA), lambda j: (0, j)),
        out_shape=jax.ShapeDtypeStruct((N_MOD_PAD, n), F32),
        compiler_params=_cparams("arbitrary"),
        name="ada",
    )(c_all, w_ada, b_ada3)


N_GATE_TILES = N_GATES // TN_IN
N_REST_TILES = N_REST // TN_IN


def _gate_col_tile(j):
    return j + N_REST_TILES


def _rest_col_tile(j):
    a_tiles = W_A // TN_IN
    return jnp.where(j < N_REST_TILES - a_tiles, j + a_tiles, j - (N_REST_TILES - a_tiles))


def _inproj_kernel(x_ref, sc_ref, sh_ref, g_ref, w_ref, b_ref, o_ref, h_scr, *, gates):
    @pl.when(pl.program_id(1) == 0)
    def _():
        y = _rms(x_ref[...]) * g_ref[...]
        h_scr[...] = (y * (1.0 + sc_ref[0]) + sh_ref[0]).astype(BF16)

    z = _dot(h_scr[...], w_ref[...]) + b_ref[0]
    o_ref[...] = _sigmoid(z).astype(BF16) if gates else z


def _inproj(x, mod3, g_pre, w_in_b, b_in3, layer, gates, tm, rows_per_mod):
    m = x.shape[0]
    r = mod3.shape[1]
    mod_row = lambda i: i // (rows_per_mod // tm)
    col = _gate_col_tile if gates else _rest_col_tile
    n_out = N_GATES if gates else N_REST
    return pl.pallas_call(
        functools.partial(_inproj_kernel, gates=gates),
        grid=(m // tm, n_out // TN_IN),
        in_specs=[
            pl.BlockSpec((tm, D_MODEL), lambda i, j: (i, 0)),
            pl.BlockSpec((1, r, D_MODEL), lambda i, j: (mod_row(i), 0, 1)),
            pl.BlockSpec((1, r, D_MODEL), lambda i, j: (mod_row(i), 0, 0)),
            pl.BlockSpec((1, D_MODEL), lambda i, j: (0, 0)),
            pl.BlockSpec((D_MODEL, TN_IN), lambda i, j: (0, col(j))),
            pl.BlockSpec((1, 1, TN_IN), lambda i, j: (layer, 0, col(j))),
        ],
        out_specs=pl.BlockSpec((tm, TN_IN), lambda i, j: (i, j)),
        out_shape=jax.ShapeDtypeStruct((m, n_out), BF16 if gates else F32),
        scratch_shapes=[pltpu.VMEM((tm, D_MODEL), BF16)],
        compiler_params=_cparams("arbitrary", "arbitrary"),
        name="inproj_gates" if gates else "inproj_rest",
    )(x, mod3, mod3, g_pre, w_in_b, b_in3)


def _inproj_sample_kernel(x_ref, sc_ref, sh_ref, g_ref, w_ref, b_ref, z_ref, h_scr):
    @pl.when(pl.program_id(0) == 0)
    def _():
        y = _rms(x_ref[...]) * g_ref[...]
        h_scr[...] = y * (1.0 + sc_ref[0]) + sh_ref[0]

    z_ref[...] = _dot32(h_scr[...], w_ref[0]) + b_ref[0]


def _inproj_sample(x, mod3, g_pre, w_in, b_in3, layer):
    n = DEC_BATCH
    return pl.pallas_call(
        _inproj_sample_kernel,
        grid=(N_IN // TN_IN,),
        in_specs=[
            pl.BlockSpec((n, D_MODEL), lambda j: (0, 0)),
            pl.BlockSpec((1, n, D_MODEL), lambda j: (0, 0, 1)),
            pl.BlockSpec((1, n, D_MODEL), lambda j: (0, 0, 0)),
            pl.BlockSpec((1, D_MODEL), lambda j: (0, 0)),
            pl.BlockSpec((1, D_MODEL, TN_IN), lambda j: (layer, 0, j)),
            pl.BlockSpec((1, 1, TN_IN), lambda j: (layer, 0, j)),
        ],
        out_specs=pl.BlockSpec((n, TN_IN), lambda j: (0, j)),
        out_shape=jax.ShapeDtypeStruct((n, N_IN), F32),
        scratch_shapes=[pltpu.VMEM((n, D_MODEL), F32)],
        compiler_params=_cparams("arbitrary"),
        name="inproj_sample",
    )(x, mod3, mod3, g_pre, w_in, b_in3)


def _mixer_prompt_kernel(uv_ref, cg_ref, a_ref, wpool_ref, bpool_ref, pscale_ref,
                         lvg_ref, lvb_ref, ws_ref, bs_ref, wdw_ref, bdw_ref, lcg_ref, lcb_ref,
                         pa_ref, sb_ref, sc_ref, npool_ref, nconv_ref,
                         aext, gext, cv_scr):
    t = pl.program_id(1)
    tt = TT_MIX

    @pl.when(t == 0)
    def _():
        aext[0:POOL_HALO, :] = jnp.zeros((POOL_HALO, W_A), F32)
        gext[0:CONV_HALO, :] = jnp.zeros((CONV_HALO, W_C), F32)

    a = a_ref[...]
    aext[POOL_HALO:POOL_HALO + tt, :] = a
    pos = t * tt + lax.broadcasted_iota(I32, (tt, 1), 0)
    for g, w in enumerate(POOL_WINDOWS):
        c0, c1 = g * POOL_GROUP, (g + 1) * POOL_GROUP
        cur = a[:, c0:c1]
        s = cur
        for j in range(1, w):
            s = s + aext[POOL_HALO - j:POOL_HALO - j + tt, c0:c1]
        cnt = jnp.minimum(w, pos + 1).astype(F32)
        d = (s / cnt - cur).astype(BF16)
        og = _dot(d, wpool_ref[g])
        pa_ref[:, c0:c1] = ((og + bpool_ref[:, c0:c1]) * pscale_ref[:, c0:c1]).astype(BF16)
    tail_a = aext[tt:tt + POOL_HALO, :]
    npool_ref[0] = tail_a
    aext[0:POOL_HALO, :] = tail_a

    uv = _gelu_tanh(uv_ref[...])
    u = uv[:, :W_B]
    vn = _layernorm(uv[:, W_B:], lvg_ref[...], lvb_ref[...]).astype(BF16)
    ri = lax.broadcasted_iota(I32, (CHUNK, CHUNK), 0)
    ci = lax.broadcasted_iota(I32, (CHUNK, CHUNK), 1)
    for h in range(H_B):
        h0, h1 = h * CHUNK, (h + 1) * CHUNK
        wsm = jnp.where(ri >= ci, ws_ref[h], 0.0).astype(BF16)
        for c in range(tt // CHUNK):
            r0, r1 = c * CHUNK, (c + 1) * CHUNK
            s = _dot(wsm, vn[r0:r1, h0:h1]) + bs_ref[:, h0:h1]
            sb_ref[r0:r1, h0:h1] = (u[r0:r1, h0:h1] * s).astype(BF16)

    cg = cg_ref[...]
    gext[CONV_HALO:CONV_HALO + tt, :] = cg[:, :W_C] * _sigmoid(cg[:, W_C:])
    rb = 64
    for cb in range(W_C // LANES):
        c0, c1 = cb * LANES, (cb + 1) * LANES
        for r in range(tt // rb):
            base = CONV_HALO - CONV_BUF + r * rb
            acc = gext[base:base + rb, c0:c1] * wdw_ref[0:1, c0:c1]
            for k in range(1, CONV_WIDTH):
                acc = acc + gext[base + k:base + k + rb, c0:c1] * wdw_ref[k:k + 1, c0:c1]
            cv_scr[r * rb:(r + 1) * rb, c0:c1] = acc
    cv = _layernorm(cv_scr[...] + bdw_ref[...], lcg_ref[...], lcb_ref[...])
    sc_ref[...] = (cv * _sigmoid(cv)).astype(BF16)
    tail_g = gext[tt:tt + CONV_HALO, :]
    nconv_ref[0] = tail_g
    gext[0:CONV_HALO, :] = tail_g


def _mixer_prompt(zrest, p):
    nt = SEQ // TT_MIX
    row = lambda b, t: b * nt + t
    vec = lambda n: pl.BlockSpec((1, n), lambda b, t: (0, 0))
    return pl.pallas_call(
        _mixer_prompt_kernel,
        grid=(BATCH, nt),
        in_specs=[
            pl.BlockSpec((TT_MIX, 2 * W_B), lambda b, t: (row(b, t), 0)),
            pl.BlockSpec((TT_MIX, 2 * W_C), lambda b, t: (row(b, t), 1)),
            pl.BlockSpec((TT_MIX, W_A), lambda b, t: (row(b, t), (2 * W_B + 2 * W_C) // W_A)),
            pl.BlockSpec((len(POOL_WINDOWS), POOL_GROUP, POOL_GROUP), lambda b, t: (0, 0, 0)),
            vec(W_A), vec(W_A), vec(W_B), vec(W_B),
            pl.BlockSpec((H_B, CHUNK, CHUNK), lambda b, t: (0, 0, 0)),
            pl.BlockSpec((CHUNK, W_B), lambda b, t: (0, 0)),
            pl.BlockSpec((CONV_WIDTH, W_C), lambda b, t: (0, 0)),
            vec(W_C), vec(W_C), vec(W_C),
        ],
        out_specs=[
            pl.BlockSpec((TT_MIX, W_A), lambda b, t: (row(b, t), 0)),
            pl.BlockSpec((TT_MIX, W_B), lambda b, t: (row(b, t), 0)),
            pl.BlockSpec((TT_MIX, W_C), lambda b, t: (row(b, t), 0)),
            pl.BlockSpec((1, POOL_HALO, W_A), lambda b, t: (b, 0, 0)),
            pl.BlockSpec((1, CONV_HALO, W_C), lambda b, t: (b, 0, 0)),
        ],
        out_shape=[
            jax.ShapeDtypeStruct((N_PROMPT, W_A), BF16),
            jax.ShapeDtypeStruct((N_PROMPT, W_B), BF16),
            jax.ShapeDtypeStruct((N_PROMPT, W_C), BF16),
            jax.ShapeDtypeStruct((BATCH, POOL_HALO, W_A), F32),
            jax.ShapeDtypeStruct((BATCH, CONV_HALO, W_C), F32),
        ],
        scratch_shapes=[
            pltpu.VMEM((TT_MIX + POOL_HALO, W_A), F32),
            pltpu.VMEM((TT_MIX + CONV_HALO, W_C), F32),
            pltpu.VMEM((TT_MIX, W_C), F32),
        ],
        compiler_params=_cparams("arbitrary", "arbitrary"),
        name="mixer_prompt",
    )(zrest, zrest, zrest, p["w_pool_b"], p["b_pool"], p["pool_scale"], p["ln_v_g"], p["ln_v_b"],
      p["w_s"], p["bs_full"], p["w_dw"], p["b_dw"], p["ln_c_g"], p["ln_c_b"])


def _mixer_sample_kernel(z_ref, spool_ref, sconv_ref, wpool_ref, bpool_ref,
                         pscale_ref, lvg_ref, lvb_ref, ws0_ref, bs0_ref, wdw_ref, bdw_ref,
                         lcg_ref, lcb_ref, pa_ref, sb_ref, sc_ref, glu_ref, v_ref, *, precise):
    rnd = (lambda t: t) if precise else (lambda t: t.astype(BF16).astype(F32))
    a = z_ref[:, 0:W_A]
    for g, w in enumerate(POOL_WINDOWS):
        c0, c1 = g * POOL_GROUP, (g + 1) * POOL_GROUP
        cur = a[:, c0:c1]
        s = cur
        for j in range(1, w):
            s = s + spool_ref[POOL_BUF - j, :, c0:c1]
        cnt = float(min(w, PAST_LEN + 1))
        og = _dot32(rnd(s / cnt - cur), rnd(wpool_ref[0, g]))
        pa_ref[:, c0:c1] = (og + bpool_ref[:, c0:c1]) * pscale_ref[:, c0:c1]

    u = _gelu_tanh(z_ref[:, W_A:W_A + W_B])
    vn = _layernorm(_gelu_tanh(z_ref[:, W_A + W_B:W_A + 2 * W_B]), lvg_ref[...], lvb_ref[...])
    v_ref[...] = vn
    sb_ref[...] = u * (rnd(ws0_ref[...]) * rnd(vn) + bs0_ref[...])

    c0 = W_A + 2 * W_B
    glu = z_ref[:, c0:c0 + W_C] * _sigmoid(z_ref[:, c0 + W_C:c0 + 2 * W_C])
    glu_ref[...] = glu
    acc = glu * wdw_ref[CONV_BUF:CONV_BUF + 1, :]
    for k in range(CONV_BUF):
        acc = acc + sconv_ref[k] * wdw_ref[k:k + 1, :]
    cv = _layernorm(acc + bdw_ref[...], lcg_ref[...], lcb_ref[...])
    sc_ref[...] = cv * _sigmoid(cv)


def _mixer_sample(z, spool_t, sconv_t, w_pool, p, layer, precise=True):
    n = DEC_BATCH
    vec = lambda w: pl.BlockSpec((1, w), lambda i: (0, 0))
    out = lambda w: pl.BlockSpec((n, w), lambda i: (0, 0))
    return pl.pallas_call(
        functools.partial(_mixer_sample_kernel, precise=precise),
        grid=(1,),
        in_specs=[
            pl.BlockSpec((n, N_REST), lambda i: (0, 0)),
            pl.BlockSpec((POOL_BUF, n, W_A), lambda i: (0, 0, 0)),
            pl.BlockSpec((CONV_BUF, n, W_C), lambda i: (0, 0, 0)),
            pl.BlockSpec((1, len(POOL_WINDOWS), POOL_GROUP, POOL_GROUP), lambda i: (layer, 0, 0, 0)),
            vec(W_A), vec(W_A), vec(W_B), vec(W_B), vec(W_B), vec(W_B),
            pl.BlockSpec((CONV_WIDTH, W_C), lambda i: (0, 0)),
            vec(W_C), vec(W_C), vec(W_C),
        ],
        out_specs=[out(W_A), out(W_B), out(W_C), out(W_C), out(W_B)],
        out_shape=[
            jax.ShapeDtypeStruct((n, W_A), F32),
            jax.ShapeDtypeStruct((n, W_B), F32),
            jax.ShapeDtypeStruct((n, W_C), F32),
            jax.ShapeDtypeStruct((n, W_C), F32),
            jax.ShapeDtypeStruct((n, W_B), F32),
        ],
        compiler_params=_cparams("arbitrary"),
        name="mixer_sample",
    )(z, spool_t, sconv_t, w_pool, p["b_pool"], p["pool_scale"],
      p["ln_v_g"], p["ln_v_b"], p["ws0"], p["bs0"], p["w_dw"], p["b_dw"], p["ln_c_g"], p["ln_c_b"])


def _route(logits, idx_ref, wgt_ref, rank_ref, cnt_ref, cnt_scr):
    tm = logits.shape[1]
    eidx = lax.broadcasted_iota(I32, (N_EXPERTS, tm), 0).astype(F32)
    work = logits
    vals, onehots = [], []
    for k in range(TOP_K):
        mx = jnp.max(work, axis=0, keepdims=True)
        ik = jnp.min(jnp.where(work == mx, eidx, float(N_EXPERTS)), axis=0, keepdims=True)
        oh = eidx == ik
        vals.append(mx)
        onehots.append(oh)
        idx_ref[k:k + 1, :] = ik.astype(I32)
        work = jnp.where(oh, -jnp.inf, work)
    ex = [jnp.exp(v - vals[0]) for v in vals]
    den = ex[0] + ex[1] + ex[2] + ex[3]
    for k in range(TOP_K):
        wgt_ref[k:k + 1, :] = ex[k] / den

    sel = jnp.zeros((N_EXPERTS, tm), F32)
    for oh in onehots:
        sel = sel + jnp.where(oh, 1.0, 0.0)
    ri = lax.broadcasted_iota(I32, (tm, tm), 0)
    ci = lax.broadcasted_iota(I32, (tm, tm), 1)
    before = jnp.where(ri < ci, 1.0, 0.0).astype(BF16)
    base = cnt_scr[:, 0:1] + _dot(sel.astype(BF16), before)
    for k in range(TOP_K):
        rank_ref[k:k + 1, :] = jnp.sum(jnp.where(onehots[k], base, 0.0), axis=0,
                                       keepdims=True).astype(I32)
    cnt_scr[...] = cnt_scr[...] + jnp.sum(sel, axis=1, keepdims=True)
    cnt_ref[...] = cnt_scr[...]


def _pack_rows(h2, hp_ref):
    hp_ref[...] = h2


def _merge_kernel(pa_ref, sb_ref, sc_ref, gates_ref, x_ref, gate1_ref, scale2_ref, shift2_ref,
                  gpost1_ref, gpre2_ref, wpa_ref, wpb_ref, wpc_ref, wo_ref, wrt_ref, brt_ref,
                  cnt_in_ref, x1_ref, hp_ref, idx_ref, wgt_ref, rank_ref, cnt_ref, cnt_scr):
    @pl.when(pl.program_id(0) == 0)
    def _():
        cnt_scr[...] = cnt_in_ref[...]

    g = gates_ref[...]
    m = g[:, :D_MODEL].astype(F32) * _dot(pa_ref[...], wpa_ref[...])
    m = m + g[:, D_MODEL:2 * D_MODEL].astype(F32) * _dot(sb_ref[...], wpb_ref[...])
    m = m + g[:, 2 * D_MODEL:].astype(F32) * _dot(sc_ref[...], wpc_ref[...])
    o = _dot(m.astype(BF16), wo_ref[...])
    x1 = x_ref[...] + gate1_ref[0] * (_rms(o) * gpost1_ref[...])
    x1_ref[...] = x1
    h2 = _rms(x1) * gpre2_ref[...] * (1.0 + scale2_ref[0]) + shift2_ref[0]
    _pack_rows(h2, hp_ref)
    logits = lax.dot_general(wrt_ref[...], h2.astype(BF16), (((1,), (1,)), ((), ())),
                             preferred_element_type=F32) + brt_ref[...]
    _route(logits, idx_ref, wgt_ref, rank_ref, cnt_ref, cnt_scr)


def _merge(pa, sb, sc, gates, x, mod3, p, cnt_in, tm, rows_per_mod):
    m = x.shape[0]
    r = mod3.shape[1]
    mod_row = lambda i: i // (rows_per_mod // tm)
    modspec = lambda k: pl.BlockSpec((1, r, D_MODEL), lambda i: (mod_row(i), 0, k))
    const = lambda shape: pl.BlockSpec(shape, lambda i: (0,) * len(shape),
                                       pipeline_mode=pl.Buffered(1))
    row = lambda w: pl.BlockSpec((tm, w), lambda i: (i, 0))
    tok = pl.BlockSpec((TOP_K, tm), lambda i: (0, i))
    return pl.pallas_call(
        _merge_kernel,
        grid=(m // tm,),
        in_specs=[
            row(W_A), row(W_B), row(W_C), row(N_GATES), row(D_MODEL),
            modspec(2), modspec(4), modspec(3),
            const((1, D_MODEL)), const((1, D_MODEL)),
            const((W_A, D_MODEL)), const((W_B, D_MODEL)), const((W_C, D_MODEL)),
            const((D_MODEL, D_MODEL)), const((N_EXPERTS, D_MODEL)), const((N_EXPERTS, 1)),
            const((N_EXPERTS, LANES)),
        ],
        out_specs=[
            row(D_MODEL),
            row(D_MODEL),
            tok, tok, tok,
            pl.BlockSpec((N_EXPERTS, LANES), lambda i: (0, 0)),
        ],
        out_shape=[
            jax.ShapeDtypeStruct((m, D_MODEL), F32),
            jax.ShapeDtypeStruct((m, D_MODEL), F32),
            jax.ShapeDtypeStruct((TOP_K, m), I32),
            jax.ShapeDtypeStruct((TOP_K, m), F32),
            jax.ShapeDtypeStruct((TOP_K, m), I32),
            jax.ShapeDtypeStruct((N_EXPERTS, LANES), F32),
        ],
        scratch_shapes=[pltpu.VMEM((N_EXPERTS, LANES), F32)],
        compiler_params=_cparams("arbitrary"),
        name="merge_router",
    )(pa, sb, sc, gates, x, mod3, mod3, mod3, p["g_post1"], p["g_pre2"],
      p["w_pa_b"], p["w_pb_b"], p["w_pc_b"], p["w_o_b"], p["w_rt_b"], p["b_rt"], cnt_in)


NT_MERGE_S = D_MODEL // TN_MERGE_S


def _merge_sample_kernel(pa_ref, sb_ref, sc_ref, ga_ref, gb_ref, gc_ref, x_ref,
                         gate1_ref, scale2_ref, shift2_ref, gpost1_ref, gpre2_ref,
                         wpa_ref, wpb_ref, wpc_ref, wo_ref, wrt_ref, brt_ref, cnt_in_ref,
                         x1_ref, hp_ref, idx_ref, wgt_ref, rank_ref, cnt_ref,
                         m_scr, o_scr, cnt_scr):
    s = pl.program_id(0)
    tn = TN_MERGE_S

    @pl.when(s < NT_MERGE_S)
    def _():
        m = _sigmoid(ga_ref[...]) * _dot32(pa_ref[...], wpa_ref[0])
        m = m + _sigmoid(gb_ref[...]) * _dot32(sb_ref[...], wpb_ref[0])
        m = m + _sigmoid(gc_ref[...]) * _dot32(sc_ref[...], wpc_ref[0])
        m_scr[jnp.minimum(s, NT_MERGE_S - 1)] = m

    @pl.when(jnp.logical_and(s >= NT_MERGE_S, s < 2 * NT_MERGE_S))
    def _():
        o = _dot32(m_scr[0], wo_ref[0, 0:tn, :])
        for k in range(1, NT_MERGE_S):
            o = o + _dot32(m_scr[k], wo_ref[0, k * tn:(k + 1) * tn, :])
        o_scr[jnp.clip(s - NT_MERGE_S, 0, NT_MERGE_S - 1)] = o

    @pl.when(s == 2 * NT_MERGE_S)
    def _():
        cnt_scr[...] = cnt_in_ref[...]
        o = jnp.concatenate([o_scr[k] for k in range(NT_MERGE_S)], axis=1)
        x1 = x_ref[...] + gate1_ref[0] * (_rms(o) * gpost1_ref[...])
        x1_ref[...] = x1
        h2 = _rms(x1) * gpre2_ref[...] * (1.0 + scale2_ref[0]) + shift2_ref[0]
        _pack_rows(h2, hp_ref)
        logits = lax.dot_general(wrt_ref[...], h2, (((1,), (1,)), ((), ())),
                                 preferred_element_type=F32, precision=HI) + brt_ref[...]
        _route(logits, idx_ref, wgt_ref, rank_ref, cnt_ref, cnt_scr)


def _merge_sample(pa, sb, sc, z, x, mod3, p, w_pa, w_pb, w_pc, w_o, cnt_in, layer):
    n = DEC_BATCH
    tn = TN_MERGE_S
    nt = NT_MERGE_S
    t1 = lambda s: jnp.minimum(s, nt - 1)
    t2 = lambda s: jnp.clip(s - nt, 0, nt - 1)
    full = lambda shape: pl.BlockSpec(shape, lambda s: (0,) * len(shape))
    modspec = lambda k: pl.BlockSpec((1, n, D_MODEL), lambda s: (0, 0, k))
    gate_cols = lambda b: pl.BlockSpec((n, tn), lambda s: (0, (N_REST + b * D_MODEL) // tn + t1(s)))
    tok = pl.BlockSpec((TOP_K, n), lambda s: (0, 0))
    return pl.pallas_call(
        _merge_sample_kernel,
        grid=(2 * nt + 1,),
        in_specs=[
            full((n, W_A)), full((n, W_B)), full((n, W_C)),
            gate_cols(0), gate_cols(1), gate_cols(2),
            full((n, D_MODEL)),
            modspec(2), modspec(4), modspec(3),
            full((1, D_MODEL)), full((1, D_MODEL)),
            pl.BlockSpec((1, W_A, tn), lambda s: (layer, 0, t1(s))),
            pl.BlockSpec((1, W_B, tn), lambda s: (layer, 0, t1(s))),
            pl.BlockSpec((1, W_C, tn), lambda s: (layer, 0, t1(s))),
            pl.BlockSpec((1, D_MODEL, tn), lambda s: (layer, 0, t2(s))),
            full((N_EXPERTS, D_MODEL)), full((N_EXPERTS, 1)), full((N_EXPERTS, LANES)),
        ],
        out_specs=[
            full((n, D_MODEL)),
            full((n, D_MODEL)),
            tok, tok, tok,
            full((N_EXPERTS, LANES)),
        ],
        out_shape=[
            jax.ShapeDtypeStruct((n, D_MODEL), F32),
            jax.ShapeDtypeStruct((n, D_MODEL), F32),
            jax.ShapeDtypeStruct((TOP_K, n), I32),
            jax.ShapeDtypeStruct((TOP_K, n), F32),
            jax.ShapeDtypeStruct((TOP_K, n), I32),
            jax.ShapeDtypeStruct((N_EXPERTS, LANES), F32),
        ],
        scratch_shapes=[
            pltpu.VMEM((nt, n, tn), F32),
            pltpu.VMEM((nt, n, tn), F32),
            pltpu.VMEM((N_EXPERTS, LANES), F32),
        ],
        compiler_params=_cparams("arbitrary"),
        name="merge_router_sample",
    )(pa, sb, sc, z, z, z, x, mod3, mod3, mod3, p["g_post1"], p["g_pre2"],
      w_pa, w_pb, w_pc, w_o, p["w_rt"], p["b_rt"], cnt_in)


def _dispatch_kernel(pos_hbm, hp_ref, hs_ref, xs_hbm, pos_smem, sem_pos, sem_rows):
    i = pl.program_id(0)
    cp = pltpu.make_async_copy(pos_hbm.at[pl.ds(i, 1)], pos_smem, sem_pos)
    cp.start()
    cp.wait()

    def scatter_rows(src_ref):
        def body(q, c):
            for dt in range(4):
                t = q * 4 + dt
                for k in range(TOP_K):
                    dst = pos_smem[0, k * TOK_TILE + t]
                    pltpu.make_async_copy(src_ref.at[pl.ds(t, 1)], xs_hbm.at[pl.ds(dst, 1)],
                                          sem_rows).start(priority=k % 2)
            return c
        lax.fori_loop(0, TOK_TILE // 4, body, 0)
        for _ in range(TOP_K):
            pltpu.make_async_copy(src_ref, xs_hbm.at[pl.ds(0, TOK_TILE)], sem_rows).wait()

    @pl.when(i < N_PROMPT // TOK_TILE)
    def _():
        scatter_rows(hp_ref)

    @pl.when(i >= N_PROMPT // TOK_TILE)
    def _():
        scatter_rows(hs_ref)


def _dispatch(pos_tiles, hp_p, hp_s):
    n_prompt_tiles = N_PROMPT // TOK_TILE
    return pl.pallas_call(
        _dispatch_kernel,
        grid=(pos_tiles.shape[0],),
        in_specs=[
            pl.BlockSpec(memory_space=pl.ANY),
            pl.BlockSpec((TOK_TILE, D_MODEL), lambda i: (jnp.minimum(i, n_prompt_tiles - 1), 0)),
            pl.BlockSpec((TOK_TILE, D_MODEL), lambda i: (0, 0)),
        ],
        out_specs=pl.BlockSpec(memory_space=pl.ANY),
        out_shape=jax.ShapeDtypeStruct((N_BLK * R_BLK, D_MODEL), F32),
        scratch_shapes=[
            pltpu.SMEM((1, TOP_K * TOK_TILE), I32),
            pltpu.SemaphoreType.DMA(()),
            pltpu.SemaphoreType.DMA(()),
        ],
        compiler_params=_cparams("arbitrary"),
        name="moe_dispatch",
    )(pos_tiles, hp_p, hp_s)


def _experts_kernel(eid_ref, rows_ref, valid_ref, ueff_ref,
                    x_hbm, wg_ref, wu_ref, bg_ref, bu_ref, wd_ref, bd_ref, y_ref,
                    xstage, xbuf, actbuf, sem_x):
    del eid_ref, valid_ref, ueff_ref
    u = pl.program_id(0)
    s = pl.program_id(1)
    nrows = rows_ref[u]
    nunits = (nrows + (R_UNIT - 1)) // R_UNIT
    live = nrows > 0

    def x_copy(blk):
        return pltpu.make_async_copy(x_hbm.at[pl.ds(blk * R_BLK, R_BLK)], xstage, sem_x)

    def for_row_class(fn):
        lo = 0
        for size in ROW_CLASSES:
            @pl.when(jnp.logical_and(nrows > lo, nrows <= size))
            def _(size=size):
                fn(size)
            lo = size

    @pl.when(jnp.logical_and(live, jnp.logical_and(u == 0, s == 0)))
    def _():
        x_copy(0).start()

    @pl.when(jnp.logical_and(live, s == 0))
    def _():
        x_copy(u).wait()

        def stage(r, c):
            r0 = pl.multiple_of(r * R_UNIT, R_UNIT)
            keep = (r0 + lax.broadcasted_iota(I32, (R_UNIT, 1), 0)) < nrows
            xbuf[pl.ds(r0, R_UNIT), :] = jnp.where(keep, xstage[pl.ds(r0, R_UNIT), :], 0.0).astype(BF16)
            return c
        lax.fori_loop(0, nunits, stage, 0)

        def zero(r, c):
            xbuf[pl.ds(pl.multiple_of(r * R_UNIT, R_UNIT), R_UNIT), :] = jnp.zeros((R_UNIT, D_MODEL), BF16)
            return c
        lax.fori_loop(nunits, R_BLK // R_UNIT, zero, 0)

    nxt = jnp.minimum(u + 1, N_BLK - 1)

    @pl.when(jnp.logical_and(s == 1, jnp.logical_and(u + 1 < N_BLK, rows_ref[nxt] > 0)))
    def _():
        x_copy(nxt).start()

    @pl.when(s < NC_UP)
    def _():
        chunk = jnp.minimum(s, NC_UP - 1)

        def up_proj(size):
            w = jnp.concatenate([wg_ref[0, 0].astype(BF16), wu_ref[0, 0].astype(BF16)], axis=1)
            gu = _dot(xbuf[0:size, :], w)
            gate = jnp.minimum(gu[:, :F_UP] + bg_ref[0, 0], SWIGLU_LIMIT)
            up = jnp.clip(gu[:, F_UP:] + bu_ref[0, 0], -SWIGLU_LIMIT, SWIGLU_LIMIT)
            act = gate * _sigmoid(SWIGLU_ALPHA * gate) * (up + 1.0)
            actbuf[chunk, 0:size, :] = act.astype(BF16)
        for_row_class(up_proj)

    @pl.when(s >= NC_UP)
    def _():
        def down_proj(size):
            acc = _dot(actbuf[0, 0:size, :], wd_ref[0, 0, 0:F_UP, :].astype(BF16))
            for k in range(1, NC_UP):
                acc = acc + _dot(actbuf[k, 0:size, :],
                                 wd_ref[0, 0, k * F_UP:(k + 1) * F_UP, :].astype(BF16))
            y_ref[0:size, :] = acc + bd_ref[0, 0]
            if size < R_BLK:
                y_ref[size:R_BLK, :] = jnp.zeros((R_BLK - size, F_DOWN), F32)
        for_row_class(down_proj)


def _experts(meta, xs, w_up, b_up4, w_down, b_down4, layer):
    eid, rows, valid, ueff = meta

    def up_chunk(u, s, valid):
        return jnp.where(valid[u] == 1, jnp.minimum(s, NC_UP - 1), NC_UP - 1)

    def down_chunk(u, s, valid):
        return jnp.where(valid[u] == 1, jnp.maximum(s - NC_UP, 0), NC_DOWN - 1)

    grid_spec = pltpu.PrefetchScalarGridSpec(
        num_scalar_prefetch=4,
        grid=(N_BLK, NC_UP + NC_DOWN),
        in_specs=[
            pl.BlockSpec(memory_space=pl.ANY),
            pl.BlockSpec((1, 1, D_MODEL, F_UP),
                         lambda u, s, e, r, v, f: (layer, e[u], 0, up_chunk(u, s, v))),
            pl.BlockSpec((1, 1, D_MODEL, F_UP),
                         lambda u, s, e, r, v, f: (layer, e[u], 0, NC_UP + up_chunk(u, s, v))),
            pl.BlockSpec((1, 1, 1, F_UP),
                         lambda u, s, e, r, v, f: (layer, e[u], 0, up_chunk(u, s, v))),
            pl.BlockSpec((1, 1, 1, F_UP),
                         lambda u, s, e, r, v, f: (layer, e[u], 0, NC_UP + up_chunk(u, s, v))),
            pl.BlockSpec((1, 1, D_FF, F_DOWN),
                         lambda u, s, e, r, v, f: (layer, e[u], 0, down_chunk(u, s, v))),
            pl.BlockSpec((1, 1, 1, F_DOWN),
                         lambda u, s, e, r, v, f: (layer, e[u], 0, down_chunk(u, s, v))),
        ],
        out_specs=pl.BlockSpec((R_BLK, F_DOWN), lambda u, s, e, r, v, f: (f[u], down_chunk(u, s, v))),
        scratch_shapes=[
            pltpu.VMEM((R_BLK, D_MODEL), F32),
            pltpu.VMEM((R_BLK, D_MODEL), BF16),
            pltpu.VMEM((NC_UP, R_BLK, F_UP), BF16),
            pltpu.SemaphoreType.DMA(()),
        ],
    )
    return pl.pallas_call(
        _experts_kernel,
        grid_spec=grid_spec,
        out_shape=jax.ShapeDtypeStruct((N_BLK * R_BLK, D_MODEL), F32),
        compiler_params=_cparams("arbitrary", "arbitrary"),
        name="moe_experts",
    )(eid, rows, valid, ueff, xs, w_up, w_up, b_up4, b_up4, w_down, b_down4)


def _combine_kernel(pos_hbm, y_hbm, w_ref, x1_ref, gate2_ref, gpost2_ref, o_ref,
                    pos_smem, gbuf, sem_pos, sem_rows, *, tile_off, sets):
    i = pl.program_id(0)
    n = pl.num_programs(0)
    slot = i % 2
    n_src = sets * TOP_K

    def pos_copy(step, sl):
        return pltpu.make_async_copy(pos_hbm.at[pl.ds(step * sets + tile_off, sets)],
                                     pos_smem.at[sl], sem_pos.at[sl])

    def issue_rows(sl):
        def body(q, c):
            for dt in range(4):
                t = q * 4 + dt
                for k in range(n_src):
                    src = pos_smem[sl, k // TOP_K, (k % TOP_K) * TOK_TILE + t]
                    pltpu.make_async_copy(y_hbm.at[pl.ds(src, 1)], gbuf.at[sl, k, pl.ds(t, 1)],
                                          sem_rows.at[sl]).start(priority=k % 2)
            return c
        lax.fori_loop(0, TOK_TILE // 4, body, 0)

    @pl.when(i == 0)
    def _():
        pos_copy(0, 0).start()
        pos_copy(0, 0).wait()
        issue_rows(0)

        @pl.when(n > 1)
        def _():
            pos_copy(1, 1).start()

    @pl.when(i + 1 < n)
    def _():
        pos_copy(i + 1, 1 - slot).wait()
        issue_rows(1 - slot)

    @pl.when(i + 2 < n)
    def _():
        pos_copy(i + 2, slot).start()

    for k in range(n_src):
        pltpu.make_async_copy(y_hbm.at[pl.ds(0, TOK_TILE)], gbuf.at[slot, k], sem_rows.at[slot]).wait()

    w = w_ref[...]
    m = w[:, 0:1] * gbuf[slot, 0]
    for k in range(1, n_src):
        m = m + w[:, k:k + 1] * gbuf[slot, k]
    o_ref[...] = x1_ref[...] + gate2_ref[0] * (_rms(m) * gpost2_ref[...])


def _combine(pos_tiles, y_sorted, wgt_t, x1, mod3, g_post2, rows_per_mod, tile_off):
    m = x1.shape[0]
    r = mod3.shape[1]
    sets = wgt_t.shape[1] // TOP_K
    mod_row = lambda i: i // (rows_per_mod // TOK_TILE)
    return pl.pallas_call(
        functools.partial(_combine_kernel, tile_off=tile_off, sets=sets),
        grid=(m // TOK_TILE,),
        in_specs=[
            pl.BlockSpec(memory_space=pl.ANY),
            pl.BlockSpec(memory_space=pl.ANY),
            pl.BlockSpec((TOK_TILE, sets * TOP_K), lambda i: (i, 0)),
            pl.BlockSpec((TOK_TILE, D_MODEL), lambda i: (i, 0)),
            pl.BlockSpec((1, r, D_MODEL), lambda i: (mod_row(i), 0, 5)),
            pl.BlockSpec((1, D_MODEL), lambda i: (0, 0)),
        ],
        out_specs=pl.BlockSpec((TOK_TILE, D_MODEL), lambda i: (i, 0)),
        out_shape=jax.ShapeDtypeStruct((m, D_MODEL), F32),
        scratch_shapes=[
            pltpu.SMEM((2, sets, TOP_K * TOK_TILE), I32),
            pltpu.VMEM((2, sets * TOP_K, TOK_TILE, D_MODEL), F32),
            pltpu.SemaphoreType.DMA((2,)),
            pltpu.SemaphoreType.DMA((2,)),
        ],
        compiler_params=_cparams("arbitrary"),
        name="moe_combine",
    )(pos_tiles, y_sorted, wgt_t, x1, mod3, g_post2)


def _route_tables(counts, idx_all, rank_all):
    nblk = (counts + (R_BLK - 1)) // R_BLK
    cum = jnp.cumsum(nblk)
    base = cum - nblk
    total = cum[-1]
    u = jnp.arange(N_BLK, dtype=I32)
    valid = u < total
    ueff = jnp.minimum(u, jnp.maximum(total - 1, 0))
    eid = jnp.minimum(jnp.sum((cum[None, :] <= ueff[:, None]).astype(I32), axis=1), N_EXPERTS - 1)
    onehot_u = (eid[:, None] == jnp.arange(N_EXPERTS, dtype=I32)[None, :]).astype(I32)
    cnt_u = jnp.sum(onehot_u * counts[None, :], axis=1)
    base_u = jnp.sum(onehot_u * base[None, :], axis=1)
    rows = jnp.where(valid, jnp.clip(cnt_u - (ueff - base_u) * R_BLK, 0, R_BLK), 0)
    e_iota = jnp.arange(N_EXPERTS, dtype=I32)[None, None, :]
    base_tok = jnp.sum(jnp.where(idx_all[:, :, None] == e_iota, base[None, None, :], 0), axis=-1)
    pos = base_tok * R_BLK + rank_all
    n_tiles = idx_all.shape[1] // TOK_TILE
    pos_tiles = pos.reshape(TOP_K, n_tiles, TOK_TILE).transpose(1, 0, 2)
    pos_tiles = pos_tiles.reshape(n_tiles, TOP_K * TOK_TILE)
    return (eid.astype(I32), rows.astype(I32), valid.astype(I32), ueff.astype(I32)), pos_tiles.astype(I32)


def _layer_params(l, w):
    row = lambda a: a[l][None, :]
    return dict(
        g_pre1=row(w["g_pre1"]), g_post1=row(w["g_post1"]),
        g_pre2=row(w["g_pre2"]), g_post2=row(w["g_post2"]),
        w_in_b=w["w_in"][l].astype(BF16),
        w_pool_b=w["w_pool"][l].astype(BF16), b_pool=row(w["b_pool"]), pool_scale=row(w["pool_scale"]),
        ln_v_g=row(w["ln_v_g"]), ln_v_b=row(w["ln_v_b"]),
        w_s=w["w_s"][l], bs_full=jnp.repeat(w["b_s"][l].T, CHUNK, axis=1),
        ws0=jnp.repeat(w["w_s"][l][:, 0, 0], CHUNK)[None, :],
        bs0=jnp.repeat(w["b_s"][l][:, 0], CHUNK)[None, :],
        w_dw=w["w_dw"][l], b_dw=row(w["b_dw"]), ln_c_g=row(w["ln_c_g"]), ln_c_b=row(w["ln_c_b"]),
        w_pa_b=w["w_pa"][l].astype(BF16), w_pb_b=w["w_pb"][l].astype(BF16),
        w_pc_b=w["w_pc"][l].astype(BF16), w_o_b=w["w_o"][l].astype(BF16),
        w_rt=w["w_router"][l].T, w_rt_b=w["w_router"][l].T.astype(BF16),
        b_rt=w["b_router"][l][:, None],
    )


def kernel(x_prompt, x_sample, c_prompt, c_sample, state_pool, state_conv, w_ada, b_ada, g_pre1, g_post1, g_pre2, g_post2, w_in, b_in, w_pool, b_pool, pool_scale, ln_v_g, ln_v_b, w_s, b_s, w_dw, b_dw, ln_c_g, ln_c_b, w_pa, w_pb, w_pc, w_o, w_router, b_router, w_up, b_up, w_down, b_down):
    weights = dict(g_pre1=g_pre1, g_post1=g_post1, g_pre2=g_pre2, g_post2=g_post2, w_in=w_in,
                   w_pool=w_pool, b_pool=b_pool, pool_scale=pool_scale, ln_v_g=ln_v_g,
                   ln_v_b=ln_v_b, w_s=w_s, b_s=b_s, w_dw=w_dw, b_dw=b_dw, ln_c_g=ln_c_g,
                   ln_c_b=ln_c_b, w_pa=w_pa, w_pb=w_pb, w_pc=w_pc, w_o=w_o, w_router=w_router,
                   b_router=b_router)
    xp = x_prompt.reshape(N_PROMPT, D_MODEL)
    xs = x_sample.reshape(DEC_BATCH, D_MODEL)
    c_all = jnp.concatenate(
        [c_sample, c_prompt, jnp.zeros((N_MOD_PAD - N_MOD, D_MODEL), F32)], axis=0)
    b_ada3 = b_ada[:, None, :]
    b_in3 = b_in[:, None, :]
    b_up4 = b_up[:, :, None, :]
    b_down4 = b_down[:, :, None, :]
    zero_cnt = jnp.zeros((N_EXPERTS, LANES), F32)

    pool_p, pool_s, conv_p, conv_s, v_s = [], [], [], [], []
    for l in range(DEPTH):
        p = _layer_params(l, weights)
        mod = _ada(c_all, w_ada, b_ada3, l)
        mod_s = mod[:DEC_BATCH][None]
        mod_p = mod[DEC_BATCH:N_MOD][:, None, :]

        in_args = (p["g_pre1"], p["w_in_b"], b_in3, l)
        gates_p = _inproj(xp, mod_p, *in_args, True, TM_IN, SEQ)
        zr_p = _inproj(xp, mod_p, *in_args, False, TM_IN, SEQ)
        z_s = _inproj_sample(xs, mod_s, p["g_pre1"], w_in, b_in3, l)
        pa_p, sb_p, sc_p, npool, nconv = _mixer_prompt(zr_p, p)
        spool_t = jnp.transpose(state_pool[l], (1, 0, 2))
        sconv_t = jnp.transpose(state_conv[l], (1, 0, 2))
        pa_s, sb_s, sc_s, glu_s, vn_s = _mixer_sample(z_s, spool_t, sconv_t, w_pool, p, l)

        x1_p, hp_p, idx_p, wgt_p, rank_p, cnt_p = _merge(
            pa_p, sb_p, sc_p, gates_p, xp, mod_p, p, zero_cnt, TM_MERGE, SEQ)
        x1_s, hp_s, idx_s, wgt_s, rank_s, cnt_s = _merge_sample(
            pa_s, sb_s, sc_s, z_s, xs, mod_s, p, w_pa, w_pb, w_pc, w_o, cnt_p, l)
        idx_parts, rank_parts = [idx_p, idx_s], [rank_p, rank_s]
        wgt_s_t, cnt_last = wgt_s.T, cnt_s

        if l == 0:
            n = DEC_BATCH
            gates_m = _inproj(xs, mod_s, *in_args, True, n, n)
            zr_m = _inproj(xs, mod_s, *in_args, False, n, n)
            z_m = jnp.concatenate([zr_m[:, N_REST - W_A:], zr_m[:, :N_REST - W_A]], axis=1)
            pa_m, sb_m, sc_m, _, _ = _mixer_sample(z_m, spool_t, sconv_t, w_pool, p, l,
                                                   precise=False)
            _, _, idx_m, wgt_m, rank_m, cnt_last = _merge(
                pa_m.astype(BF16), sb_m.astype(BF16), sc_m.astype(BF16), gates_m, xs, mod_s, p,
                cnt_s, n, n)
            idx_parts.append(idx_m)
            rank_parts.append(rank_m)
            wgt_s_t = 0.5 * jnp.concatenate([wgt_s.T, wgt_m.T], axis=1)

        counts = cnt_last[:, 0].astype(I32)
        idx_all = jnp.concatenate(idx_parts, axis=1)
        rank_all = jnp.concatenate(rank_parts, axis=1)
        meta, pos_tiles = _route_tables(counts, idx_all, rank_all)
        x_sorted = _dispatch(pos_tiles, hp_p, hp_s)
        y_sorted = _experts(meta, x_sorted, w_up, b_up4, w_down, b_down4, l)
        xp = _combine(pos_tiles, y_sorted, wgt_p.T, x1_p, mod_p, p["g_post2"], SEQ, 0)
        xs = _combine(pos_tiles, y_sorted, wgt_s_t, x1_s, mod_s, p["g_post2"], DEC_BATCH,
                      N_PROMPT // TOK_TILE)

        pool_p.append(npool[:, POOL_HALO - POOL_BUF:])
        conv_p.append(nconv[:, CONV_HALO - CONV_BUF:])
        pool_s.append(jnp.concatenate([state_pool[l][:, 1:], z_s[:, None, :W_A]], axis=1))
        conv_s.append(jnp.concatenate([state_conv[l][:, 1:], glu_s[:, None, :]], axis=1))
        v_s.append(vn_s[:, None, :])

    return (xp.reshape(BATCH, SEQ, D_MODEL), xs.reshape(DEC_BATCH, 1, D_MODEL),
            jnp.stack(pool_p), jnp.stack(pool_s), jnp.stack(conv_p), jnp.stack(conv_s),
            jnp.stack(v_s))
```

```python
import functools

import jax
import jax.numpy as jnp
from jax import lax
from jax.experimental import pallas as pl
from jax.experimental.pallas import tpu as pltpu

F32 = jnp.float32
BF16 = jnp.bfloat16
I32 = jnp.int32

D_MODEL = 2048
BATCH = 4
SEQ = 2048
DEPTH = 2
DEC_BATCH = 128
PAST_LEN = 16384
W_A = 512
POOL_WINDOWS = (2, 4, 8, 16)
POOL_GROUP = 128
POOL_BUF = 15
W_B = 768
CHUNK = 128
H_B = 6
W_C = 768
CONV_WIDTH = 31
CONV_BUF = 30
N_GATES = 3 * D_MODEL
N_REST = W_A + 2 * W_B + 2 * W_C
N_IN = N_REST + N_GATES
N_EXPERTS = 32
TOP_K = 4
D_FF = D_MODEL
SWIGLU_LIMIT = 7.0
SWIGLU_ALPHA = 1.702
EPS = 1e-6

N_PROMPT = BATCH * SEQ
N_TOK = N_PROMPT + 2 * DEC_BATCH
N_MOD = DEC_BATCH + BATCH
N_MOD_PAD = 136

LANES = 128
SUBLANES = 8

VMEM_LIMIT = 56 * 1024 * 1024
VMEM_LIMIT_EXPERTS = 61 * 1024 * 1024

TN_ADA = 1024
TM_IN = 1024
TN_IN = 512
TN_GATES = 1024
TT_MIX = 256
POOL_HALO = 16
CONV_HALO = 32
TM_MERGE = 256
TN_MERGE_S = 512
TOK_TILE = 128
R_BLK = 1536
R_UNIT = 128
ROW_CLASSES = (896, 1152, R_BLK)
F_UP = 512
F_ACT = 256
F_DOWN = 512
NC_UP = D_FF // F_UP
NC_ACT = D_FF // F_ACT
NC_DOWN = D_MODEL // F_DOWN
N_BLK = N_EXPERTS + (N_TOK * TOP_K) // R_BLK

HI = lax.Precision.HIGHEST


def _cparams(*sem, vmem=VMEM_LIMIT):
    return pltpu.CompilerParams(dimension_semantics=sem, vmem_limit_bytes=vmem)


def _sigmoid(x):
    return 1.0 / (1.0 + jnp.exp(-x))


def _gelu_tanh(x):
    c = 0.7978845608028654
    return x * (0.5 * (1.0 + jnp.tanh(c * (x + 0.044715 * (x * x * x)))))


def _rms(x):
    return x * lax.rsqrt(jnp.mean(x * x, axis=-1, keepdims=True) + EPS)


def _layernorm(x, g, b):
    mu = jnp.mean(x, axis=-1, keepdims=True)
    xc = x - mu
    return xc * lax.rsqrt(jnp.mean(xc * xc, axis=-1, keepdims=True) + EPS) * g + b


def _dot(a, b):
    return jnp.dot(a, b, preferred_element_type=F32)


def _dot32(a, b):
    return jnp.dot(a, b, preferred_element_type=F32, precision=HI)


def _ada_kernel(c_ref, w_ref, b_ref, o_ref):
    c = c_ref[...]
    s = (c * _sigmoid(c)).astype(BF16)
    o_ref[...] = _dot(s, w_ref[0].astype(BF16)) + b_ref[0]


def _ada(c_all, w_ada, b_ada3, layer):
    n = 6 * D_MODEL
    return pl.pallas_call(
        _ada_kernel,
        grid=(n // TN_ADA,),
        in_specs=[
            pl.BlockSpec((N_MOD_PAD, D_MODEL), lambda j: (0, 0)),
            pl.BlockSpec((1, D_MODEL, TN_ADA), lambda j: (layer, 0, j)),
            pl.BlockSpec((1, 1, TN_ADA), lambda j: (layer, 0, j)),
        ],
        out_specs=pl.BlockSpec((N_MOD_PAD, TN_ADA), lambda j: (0, j)),
        out_shape=jax.ShapeDtypeStruct((N_MOD_PAD, n), F32),
        compiler_params=_cparams("arbitrary"),
        name="ada",
    )(c_all, w_ada, b_ada3)


N_REST_TILES = N_REST // TN_IN


def _rest_col_tile(j):
    a_tiles = W_A // TN_IN
    return jnp.where(j < N_REST_TILES - a_tiles, j + a_tiles, j - (N_REST_TILES - a_tiles))


def _inproj_kernel(x_ref, sc_ref, sh_ref, g_ref, w_ref, b_ref, o_ref, h_scr, *, gates):
    @pl.when(pl.program_id(1) == 0)
    def _():
        y = _rms(x_ref[...]) * g_ref[...]
        h_scr[...] = (y * (1.0 + sc_ref[0]) + sh_ref[0]).astype(BF16)

    z = _dot(h_scr[...], w_ref[...]) + b_ref[0]
    o_ref[...] = _sigmoid(z).astype(BF16) if gates else z


def _inproj(x, mod3, g_pre, w_b, b3, layer, gates, tm, rows_per_mod):
    m = x.shape[0]
    r = mod3.shape[1]
    mod_row = lambda i: i // (rows_per_mod // tm)
    col = (lambda j: j) if gates else _rest_col_tile
    n_out = N_GATES if gates else N_REST
    tn = TN_GATES if gates else TN_IN
    return pl.pallas_call(
        functools.partial(_inproj_kernel, gates=gates),
        grid=(m // tm, n_out // tn),
        in_specs=[
            pl.BlockSpec((tm, D_MODEL), lambda i, j: (i, 0)),
            pl.BlockSpec((1, r, D_MODEL), lambda i, j: (mod_row(i), 0, 1)),
            pl.BlockSpec((1, r, D_MODEL), lambda i, j: (mod_row(i), 0, 0)),
            pl.BlockSpec((1, D_MODEL), lambda i, j: (0, 0)),
            pl.BlockSpec((D_MODEL, tn), lambda i, j: (0, col(j))),
            pl.BlockSpec((1, 1, tn), lambda i, j: (layer, 0, col(j))),
        ],
        out_specs=pl.BlockSpec((tm, tn), lambda i, j: (i, j)),
        out_shape=jax.ShapeDtypeStruct((m, n_out), BF16 if gates else F32),
        scratch_shapes=[pltpu.VMEM((tm, D_MODEL), BF16)],
        compiler_params=_cparams("arbitrary", "arbitrary"),
        name="inproj_gates" if gates else "inproj_rest",
    )(x, mod3, mod3, g_pre, w_b, b3)


def _inproj_sample_kernel(x_ref, sc_ref, sh_ref, g_ref, w_ref, b_ref, z_ref, h_scr):
    @pl.when(pl.program_id(0) == 0)
    def _():
        y = _rms(x_ref[...]) * g_ref[...]
        h_scr[...] = y * (1.0 + sc_ref[0]) + sh_ref[0]

    z_ref[...] = _dot32(h_scr[...], w_ref[0]) + b_ref[0]


def _inproj_sample(x, mod3, g_pre, w_in, b_in3, layer):
    n = DEC_BATCH
    return pl.pallas_call(
        _inproj_sample_kernel,
        grid=(N_IN // TN_IN,),
        in_specs=[
            pl.BlockSpec((n, D_MODEL), lambda j: (0, 0)),
            pl.BlockSpec((1, n, D_MODEL), lambda j: (0, 0, 1)),
            pl.BlockSpec((1, n, D_MODEL), lambda j: (0, 0, 0)),
            pl.BlockSpec((1, D_MODEL), lambda j: (0, 0)),
            pl.BlockSpec((1, D_MODEL, TN_IN), lambda j: (layer, 0, j)),
            pl.BlockSpec((1, 1, TN_IN), lambda j: (layer, 0, j)),
        ],
        out_specs=pl.BlockSpec((n, TN_IN), lambda j: (0, j)),
        out_shape=jax.ShapeDtypeStruct((n, N_IN), F32),
        scratch_shapes=[pltpu.VMEM((n, D_MODEL), F32)],
        compiler_params=_cparams("arbitrary"),
        name="inproj_sample",
    )(x, mod3, mod3, g_pre, w_in, b_in3)


def _mixer_prompt_kernel(uv_ref, cg_ref, a_ref, wpool_ref, bpool_ref, pscale_ref,
                         lvg_ref, lvb_ref, ws_ref, bs_ref, wdw_ref, bdw_ref, lcg_ref, lcb_ref,
                         pa_ref, sb_ref, sc_ref, npool_ref, nconv_ref,
                         aext, gext, cv_scr):
    t = pl.program_id(1)
    tt = TT_MIX

    @pl.when(t == 0)
    def _():
        aext[0:POOL_HALO, :] = jnp.zeros((POOL_HALO, W_A), F32)
        gext[0:CONV_HALO, :] = jnp.zeros((CONV_HALO, W_C), F32)

    a = a_ref[...]
    aext[POOL_HALO:POOL_HALO + tt, :] = a
    pos = t * tt + lax.broadcasted_iota(I32, (tt, 1), 0)
    for g, w in enumerate(POOL_WINDOWS):
        c0, c1 = g * POOL_GROUP, (g + 1) * POOL_GROUP
        cur = a[:, c0:c1]
        s = cur
        for j in range(1, w):
            s = s + aext[POOL_HALO - j:POOL_HALO - j + tt, c0:c1]
        cnt = jnp.minimum(w, pos + 1).astype(F32)
        d = (s / cnt - cur).astype(BF16)
        og = _dot(d, wpool_ref[g])
        pa_ref[:, c0:c1] = ((og + bpool_ref[:, c0:c1]) * pscale_ref[:, c0:c1]).astype(BF16)
    tail_a = aext[tt:tt + POOL_HALO, :]
    npool_ref[0] = tail_a
    aext[0:POOL_HALO, :] = tail_a

    uv = _gelu_tanh(uv_ref[...])
    u = uv[:, :W_B]
    vn = _layernorm(uv[:, W_B:], lvg_ref[...], lvb_ref[...]).astype(BF16)
    ri = lax.broadcasted_iota(I32, (CHUNK, CHUNK), 0)
    ci = lax.broadcasted_iota(I32, (CHUNK, CHUNK), 1)
    for h in range(H_B):
        h0, h1 = h * CHUNK, (h + 1) * CHUNK
        wsm = jnp.where(ri >= ci, ws_ref[h], 0.0).astype(BF16)
        for c in range(tt // CHUNK):
            r0, r1 = c * CHUNK, (c + 1) * CHUNK
            s = _dot(wsm, vn[r0:r1, h0:h1]) + bs_ref[:, h0:h1]
            sb_ref[r0:r1, h0:h1] = (u[r0:r1, h0:h1] * s).astype(BF16)

    cg = cg_ref[...]
    gext[CONV_HALO:CONV_HALO + tt, :] = cg[:, :W_C] * _sigmoid(cg[:, W_C:])
    rb = 64
    for cb in range(W_C // LANES):
        c0, c1 = cb * LANES, (cb + 1) * LANES
        for r in range(tt // rb):
            base = CONV_HALO - CONV_BUF + r * rb
            acc = gext[base:base + rb, c0:c1] * wdw_ref[0:1, c0:c1]
            for k in range(1, CONV_WIDTH):
                acc = acc + gext[base + k:base + k + rb, c0:c1] * wdw_ref[k:k + 1, c0:c1]
            cv_scr[r * rb:(r + 1) * rb, c0:c1] = acc
    cv = _layernorm(cv_scr[...] + bdw_ref[...], lcg_ref[...], lcb_ref[...])
    sc_ref[...] = (cv * _sigmoid(cv)).astype(BF16)
    tail_g = gext[tt:tt + CONV_HALO, :]
    nconv_ref[0] = tail_g
    gext[0:CONV_HALO, :] = tail_g


def _mixer_prompt(zrest, p):
    nt = SEQ // TT_MIX
    row = lambda b, t: b * nt + t
    vec = lambda n: pl.BlockSpec((1, n), lambda b, t: (0, 0))
    return pl.pallas_call(
        _mixer_prompt_kernel,
        grid=(BATCH, nt),
        in_specs=[
            pl.BlockSpec((TT_MIX, 2 * W_B), lambda b, t: (row(b, t), 0)),
            pl.BlockSpec((TT_MIX, 2 * W_C), lambda b, t: (row(b, t), 1)),
            pl.BlockSpec((TT_MIX, W_A), lambda b, t: (row(b, t), (2 * W_B + 2 * W_C) // W_A)),
            pl.BlockSpec((len(POOL_WINDOWS), POOL_GROUP, POOL_GROUP), lambda b, t: (0, 0, 0)),
            vec(W_A), vec(W_A), vec(W_B), vec(W_B),
            pl.BlockSpec((H_B, CHUNK, CHUNK), lambda b, t: (0, 0, 0)),
            pl.BlockSpec((CHUNK, W_B), lambda b, t: (0, 0)),
            pl.BlockSpec((CONV_WIDTH, W_C), lambda b, t: (0, 0)),
            vec(W_C), vec(W_C), vec(W_C),
        ],
        out_specs=[
            pl.BlockSpec((TT_MIX, W_A), lambda b, t: (row(b, t), 0)),
            pl.BlockSpec((TT_MIX, W_B), lambda b, t: (row(b, t), 0)),
            pl.BlockSpec((TT_MIX, W_C), lambda b, t: (row(b, t), 0)),
            pl.BlockSpec((1, POOL_HALO, W_A), lambda b, t: (b, 0, 0)),
            pl.BlockSpec((1, CONV_HALO, W_C), lambda b, t: (b, 0, 0)),
        ],
        out_shape=[
            jax.ShapeDtypeStruct((N_PROMPT, W_A), BF16),
            jax.ShapeDtypeStruct((N_PROMPT, W_B), BF16),
            jax.ShapeDtypeStruct((N_PROMPT, W_C), BF16),
            jax.ShapeDtypeStruct((BATCH, POOL_HALO, W_A), F32),
            jax.ShapeDtypeStruct((BATCH, CONV_HALO, W_C), F32),
        ],
        scratch_shapes=[
            pltpu.VMEM((TT_MIX + POOL_HALO, W_A), F32),
            pltpu.VMEM((TT_MIX + CONV_HALO, W_C), F32),
            pltpu.VMEM((TT_MIX, W_C), F32),
        ],
        compiler_params=_cparams("arbitrary", "arbitrary"),
        name="mixer_prompt",
    )(zrest, zrest, zrest, p["w_pool_b"], p["b_pool"], p["pool_scale"], p["ln_v_g"], p["ln_v_b"],
      p["w_s"], p["bs_full"], p["w_dw"], p["b_dw"], p["ln_c_g"], p["ln_c_b"])


def _mixer_sample_kernel(z_ref, spool_ref, sconv_ref, wpool_ref, bpool_ref,
                         pscale_ref, lvg_ref, lvb_ref, ws0_ref, bs0_ref, wdw_ref, bdw_ref,
                         lcg_ref, lcb_ref, pa_ref, sb_ref, sc_ref, glu_ref, v_ref, *, precise):
    rnd = (lambda t: t) if precise else (lambda t: t.astype(BF16).astype(F32))
    a = z_ref[:, 0:W_A]
    for g, w in enumerate(POOL_WINDOWS):
        c0, c1 = g * POOL_GROUP, (g + 1) * POOL_GROUP
        cur = a[:, c0:c1]
        s = cur
        for j in range(1, w):
            s = s + spool_ref[POOL_BUF - j, :, c0:c1]
        cnt = float(min(w, PAST_LEN + 1))
        og = _dot32(rnd(s / cnt - cur), rnd(wpool_ref[0, g]))
        pa_ref[:, c0:c1] = (og + bpool_ref[:, c0:c1]) * pscale_ref[:, c0:c1]

    u = _gelu_tanh(z_ref[:, W_A:W_A + W_B])
    vn = _layernorm(_gelu_tanh(z_ref[:, W_A + W_B:W_A + 2 * W_B]), lvg_ref[...], lvb_ref[...])
    v_ref[...] = vn
    sb_ref[...] = u * (rnd(ws0_ref[...]) * rnd(vn) + bs0_ref[...])

    c0 = W_A + 2 * W_B
    glu = z_ref[:, c0:c0 + W_C] * _sigmoid(z_ref[:, c0 + W_C:c0 + 2 * W_C])
    glu_ref[...] = glu
    acc = glu * wdw_ref[CONV_BUF:CONV_BUF + 1, :]
    for k in range(CONV_BUF):
        acc = acc + sconv_ref[k] * wdw_ref[k:k + 1, :]
    cv = _layernorm(acc + bdw_ref[...], lcg_ref[...], lcb_ref[...])
    sc_ref[...] = cv * _sigmoid(cv)


def _mixer_sample(z, spool_t, sconv_t, w_pool, p, layer, precise=True):
    n = DEC_BATCH
    vec = lambda w: pl.BlockSpec((1, w), lambda i: (0, 0))
    out = lambda w: pl.BlockSpec((n, w), lambda i: (0, 0))
    return pl.pallas_call(
        functools.partial(_mixer_sample_kernel, precise=precise),
        grid=(1,),
        in_specs=[
            pl.BlockSpec((n, N_REST), lambda i: (0, 0)),
            pl.BlockSpec((POOL_BUF, n, W_A), lambda i: (0, 0, 0)),
            pl.BlockSpec((CONV_BUF, n, W_C), lambda i: (0, 0, 0)),
            pl.BlockSpec((1, len(POOL_WINDOWS), POOL_GROUP, POOL_GROUP), lambda i: (layer, 0, 0, 0)),
            vec(W_A), vec(W_A), vec(W_B), vec(W_B), vec(W_B), vec(W_B),
            pl.BlockSpec((CONV_WIDTH, W_C), lambda i: (0, 0)),
            vec(W_C), vec(W_C), vec(W_C),
        ],
        out_specs=[out(W_A), out(W_B), out(W_C), out(W_C), out(W_B)],
        out_shape=[
            jax.ShapeDtypeStruct((n, W_A), F32),
            jax.ShapeDtypeStruct((n, W_B), F32),
            jax.ShapeDtypeStruct((n, W_C), F32),
            jax.ShapeDtypeStruct((n, W_C), F32),
            jax.ShapeDtypeStruct((n, W_B), F32),
        ],
        compiler_params=_cparams("arbitrary"),
        name="mixer_sample",
    )(z, spool_t, sconv_t, w_pool, p["b_pool"], p["pool_scale"],
      p["ln_v_g"], p["ln_v_b"], p["ws0"], p["bs0"], p["w_dw"], p["b_dw"], p["ln_c_g"], p["ln_c_b"])


def _route(logits, idx_ref, wgt_ref, rank_ref, cnt_ref, cnt_scr):
    tm = logits.shape[1]
    eidx = lax.broadcasted_iota(I32, (N_EXPERTS, tm), 0).astype(F32)
    work = logits
    vals, onehots = [], []
    for k in range(TOP_K):
        mx = jnp.max(work, axis=0, keepdims=True)
        ik = jnp.min(jnp.where(work == mx, eidx, float(N_EXPERTS)), axis=0, keepdims=True)
        oh = eidx == ik
        vals.append(mx)
        onehots.append(oh)
        idx_ref[k:k + 1, :] = ik.astype(I32)
        work = jnp.where(oh, -jnp.inf, work)
    ex = [jnp.exp(v - vals[0]) for v in vals]
    den = ex[0] + ex[1] + ex[2] + ex[3]
    for k in range(TOP_K):
        wgt_ref[k:k + 1, :] = ex[k] / den

    sel = jnp.zeros((N_EXPERTS, tm), F32)
    for oh in onehots:
        sel = sel + jnp.where(oh, 1.0, 0.0)
    ri = lax.broadcasted_iota(I32, (tm, tm), 0)
    ci = lax.broadcasted_iota(I32, (tm, tm), 1)
    before = jnp.where(ri < ci, 1.0, 0.0).astype(BF16)
    base = cnt_scr[:, 0:1] + _dot(sel.astype(BF16), before)
    for k in range(TOP_K):
        rank_ref[k:k + 1, :] = jnp.sum(jnp.where(onehots[k], base, 0.0), axis=0,
                                       keepdims=True).astype(I32)
    cnt_scr[...] = cnt_scr[...] + jnp.sum(sel, axis=1, keepdims=True)
    cnt_ref[...] = cnt_scr[...]


def _pack_rows(h2, hp_ref):
    hp_ref[...] = h2


def _merge_kernel(pa_ref, sb_ref, sc_ref, gates_ref, x_ref, gate1_ref, scale2_ref, shift2_ref,
                  gpost1_ref, gpre2_ref, wpa_ref, wpb_ref, wpc_ref, wo_ref, wrt_ref, brt_ref,
                  cnt_in_ref, x1_ref, hp_ref, idx_ref, wgt_ref, rank_ref, cnt_ref, cnt_scr):
    @pl.when(pl.program_id(0) == 0)
    def _():
        cnt_scr[...] = cnt_in_ref[...]

    g = gates_ref[...]
    m = g[:, :D_MODEL].astype(F32) * _dot(pa_ref[...], wpa_ref[...])
    m = m + g[:, D_MODEL:2 * D_MODEL].astype(F32) * _dot(sb_ref[...], wpb_ref[...])
    m = m + g[:, 2 * D_MODEL:].astype(F32) * _dot(sc_ref[...], wpc_ref[...])
    o = _dot(m.astype(BF16), wo_ref[...])
    x1 = x_ref[...] + gate1_ref[0] * (_rms(o) * gpost1_ref[...])
    x1_ref[...] = x1
    h2 = _rms(x1) * gpre2_ref[...] * (1.0 + scale2_ref[0]) + shift2_ref[0]
    _pack_rows(h2, hp_ref)
    logits = lax.dot_general(wrt_ref[...], h2.astype(BF16), (((1,), (1,)), ((), ())),
                             preferred_element_type=F32) + brt_ref[...]
    _route(logits, idx_ref, wgt_ref, rank_ref, cnt_ref, cnt_scr)


def _merge(pa, sb, sc, gates, x, mod3, p, cnt_in, tm, rows_per_mod):
    m = x.shape[0]
    r = mod3.shape[1]
    mod_row = lambda i: i // (rows_per_mod // tm)
    modspec = lambda k: pl.BlockSpec((1, r, D_MODEL), lambda i: (mod_row(i), 0, k))
    const = lambda shape: pl.BlockSpec(shape, lambda i: (0,) * len(shape),
                                       pipeline_mode=pl.Buffered(1))
    row = lambda w: pl.BlockSpec((tm, w), lambda i: (i, 0))
    tok = pl.BlockSpec((TOP_K, tm), lambda i: (0, i))
    return pl.pallas_call(
        _merge_kernel,
        grid=(m // tm,),
        in_specs=[
            row(W_A), row(W_B), row(W_C), row(N_GATES), row(D_MODEL),
            modspec(2), modspec(4), modspec(3),
            const((1, D_MODEL)), const((1, D_MODEL)),
            const((W_A, D_MODEL)), const((W_B, D_MODEL)), const((W_C, D_MODEL)),
            const((D_MODEL, D_MODEL)), const((N_EXPERTS, D_MODEL)), const((N_EXPERTS, 1)),
            const((N_EXPERTS, LANES)),
        ],
        out_specs=[
            row(D_MODEL),
            row(D_MODEL),
            tok, tok, tok,
            pl.BlockSpec((N_EXPERTS, LANES), lambda i: (0, 0)),
        ],
        out_shape=[
            jax.ShapeDtypeStruct((m, D_MODEL), F32),
            jax.ShapeDtypeStruct((m, D_MODEL), F32),
            jax.ShapeDtypeStruct((TOP_K, m), I32),
            jax.ShapeDtypeStruct((TOP_K, m), F32),
            jax.ShapeDtypeStruct((TOP_K, m), I32),
            jax.ShapeDtypeStruct((N_EXPERTS, LANES), F32),
        ],
        scratch_shapes=[pltpu.VMEM((N_EXPERTS, LANES), F32)],
        compiler_params=_cparams("arbitrary"),
        name="merge_router",
    )(pa, sb, sc, gates, x, mod3, mod3, mod3, p["g_post1"], p["g_pre2"],
      p["w_pa_b"], p["w_pb_b"], p["w_pc_b"], p["w_o_b"], p["w_rt_b"], p["b_rt"], cnt_in)


NT_MERGE_S = D_MODEL // TN_MERGE_S


def _merge_sample_kernel(pa_ref, sb_ref, sc_ref, ga_ref, gb_ref, gc_ref, x_ref,
                         gate1_ref, scale2_ref, shift2_ref, gpost1_ref, gpre2_ref,
                         wpa_ref, wpb_ref, wpc_ref, wo_ref, wrt_ref, brt_ref, cnt_in_ref,
                         x1_ref, hp_ref, idx_ref, wgt_ref, rank_ref, cnt_ref,
                         m_scr, o_scr, cnt_scr):
    s = pl.program_id(0)
    tn = TN_MERGE_S

    @pl.when(s < NT_MERGE_S)
    def _():
        m = _sigmoid(ga_ref[...]) * _dot32(pa_ref[...], wpa_ref[0])
        m = m + _sigmoid(gb_ref[...]) * _dot32(sb_ref[...], wpb_ref[0])
        m = m + _sigmoid(gc_ref[...]) * _dot32(sc_ref[...], wpc_ref[0])
        m_scr[jnp.minimum(s, NT_MERGE_S - 1)] = m

    @pl.when(jnp.logical_and(s >= NT_MERGE_S, s < 2 * NT_MERGE_S))
    def _():
        o = _dot32(m_scr[0], wo_ref[0, 0:tn, :])
        for k in range(1, NT_MERGE_S):
            o = o + _dot32(m_scr[k], wo_ref[0, k * tn:(k + 1) * tn, :])
        o_scr[jnp.clip(s - NT_MERGE_S, 0, NT_MERGE_S - 1)] = o

    @pl.when(s == 2 * NT_MERGE_S)
    def _():
        cnt_scr[...] = cnt_in_ref[...]
        o = jnp.concatenate([o_scr[k] for k in range(NT_MERGE_S)], axis=1)
        x1 = x_ref[...] + gate1_ref[0] * (_rms(o) * gpost1_ref[...])
        x1_ref[...] = x1
        h2 = _rms(x1) * gpre2_ref[...] * (1.0 + scale2_ref[0]) + shift2_ref[0]
        _pack_rows(h2, hp_ref)
        logits = lax.dot_general(wrt_ref[...], h2, (((1,), (1,)), ((), ())),
                                 preferred_element_type=F32, precision=HI) + brt_ref[...]
        _route(logits, idx_ref, wgt_ref, rank_ref, cnt_ref, cnt_scr)


def _merge_sample(pa, sb, sc, z, x, mod3, p, w_pa, w_pb, w_pc, w_o, cnt_in, layer):
    n = DEC_BATCH
    tn = TN_MERGE_S
    nt = NT_MERGE_S
    t1 = lambda s: jnp.minimum(s, nt - 1)
    t2 = lambda s: jnp.clip(s - nt, 0, nt - 1)
    full = lambda shape: pl.BlockSpec(shape, lambda s: (0,) * len(shape))
    modspec = lambda k: pl.BlockSpec((1, n, D_MODEL), lambda s: (0, 0, k))
    gate_cols = lambda b: pl.BlockSpec((n, tn), lambda s: (0, (N_REST + b * D_MODEL) // tn + t1(s)))
    tok = pl.BlockSpec((TOP_K, n), lambda s: (0, 0))
    return pl.pallas_call(
        _merge_sample_kernel,
        grid=(2 * nt + 1,),
        in_specs=[
            full((n, W_A)), full((n, W_B)), full((n, W_C)),
            gate_cols(0), gate_cols(1), gate_cols(2),
            full((n, D_MODEL)),
            modspec(2), modspec(4), modspec(3),
            full((1, D_MODEL)), full((1, D_MODEL)),
            pl.BlockSpec((1, W_A, tn), lambda s: (layer, 0, t1(s))),
            pl.BlockSpec((1, W_B, tn), lambda s: (layer, 0, t1(s))),
            pl.BlockSpec((1, W_C, tn), lambda s: (layer, 0, t1(s))),
            pl.BlockSpec((1, D_MODEL, tn), lambda s: (layer, 0, t2(s))),
            full((N_EXPERTS, D_MODEL)), full((N_EXPERTS, 1)), full((N_EXPERTS, LANES)),
        ],
        out_specs=[
            full((n, D_MODEL)),
            full((n, D_MODEL)),
            tok, tok, tok,
            full((N_EXPERTS, LANES)),
        ],
        out_shape=[
            jax.ShapeDtypeStruct((n, D_MODEL), F32),
            jax.ShapeDtypeStruct((n, D_MODEL), F32),
            jax.ShapeDtypeStruct((TOP_K, n), I32),
            jax.ShapeDtypeStruct((TOP_K, n), F32),
            jax.ShapeDtypeStruct((TOP_K, n), I32),
            jax.ShapeDtypeStruct((N_EXPERTS, LANES), F32),
        ],
        scratch_shapes=[
            pltpu.VMEM((nt, n, tn), F32),
            pltpu.VMEM((nt, n, tn), F32),
            pltpu.VMEM((N_EXPERTS, LANES), F32),
        ],
        compiler_params=_cparams("arbitrary"),
        name="merge_router_sample",
    )(pa, sb, sc, z, z, z, x, mod3, mod3, mod3, p["g_post1"], p["g_pre2"],
      w_pa, w_pb, w_pc, w_o, p["w_rt"], p["b_rt"], cnt_in)


def _dispatch_kernel(pos_hbm, hp_ref, hs_ref, xs_hbm, pos_smem, sem_pos, sem_rows):
    i = pl.program_id(0)
    cp = pltpu.make_async_copy(pos_hbm.at[pl.ds(i, 1)], pos_smem, sem_pos)
    cp.start()
    cp.wait()

    def scatter_rows(src_ref):
        def body(q, c):
            for dt in range(4):
                t = q * 4 + dt
                for k in range(TOP_K):
                    dst = pos_smem[0, k * TOK_TILE + t]
                    pltpu.make_async_copy(src_ref.at[pl.ds(t, 1)], xs_hbm.at[pl.ds(dst, 1)],
                                          sem_rows).start(priority=k % 2)
            return c
        lax.fori_loop(0, TOK_TILE // 4, body, 0)
        for _ in range(TOP_K):
            pltpu.make_async_copy(src_ref, xs_hbm.at[pl.ds(0, TOK_TILE)], sem_rows).wait()

    @pl.when(i < N_PROMPT // TOK_TILE)
    def _():
        scatter_rows(hp_ref)

    @pl.when(i >= N_PROMPT // TOK_TILE)
    def _():
        scatter_rows(hs_ref)


def _dispatch(pos_tiles, hp_p, hp_s):
    n_prompt_tiles = N_PROMPT // TOK_TILE
    return pl.pallas_call(
        _dispatch_kernel,
        grid=(pos_tiles.shape[0],),
        in_specs=[
            pl.BlockSpec(memory_space=pl.ANY),
            pl.BlockSpec((TOK_TILE, D_MODEL), lambda i: (jnp.minimum(i, n_prompt_tiles - 1), 0)),
            pl.BlockSpec((TOK_TILE, D_MODEL), lambda i: (0, 0)),
        ],
        out_specs=pl.BlockSpec(memory_space=pl.ANY),
        out_shape=jax.ShapeDtypeStruct((N_BLK * R_BLK, D_MODEL), F32),
        scratch_shapes=[
            pltpu.SMEM((1, TOP_K * TOK_TILE), I32),
            pltpu.SemaphoreType.DMA(()),
            pltpu.SemaphoreType.DMA(()),
        ],
        compiler_params=_cparams("arbitrary"),
        name="moe_dispatch",
    )(pos_tiles, hp_p, hp_s)


def _experts_kernel(eid_ref, rows_ref, valid_ref, ueff_ref,
                    x_hbm, wg_ref, wu_ref, bg_ref, bu_ref, wd_ref, bd_ref, y_ref,
                    xstage, xbuf, actbuf, sem_x):
    del eid_ref, valid_ref, ueff_ref
    u = pl.program_id(0)
    s = pl.program_id(1)
    nrows = rows_ref[u]
    nunits = (nrows + (R_UNIT - 1)) // R_UNIT
    live = nrows > 0

    def x_copy(blk):
        return pltpu.make_async_copy(x_hbm.at[pl.ds(blk * R_BLK, R_BLK)], xstage, sem_x)

    def for_row_class(fn):
        lo = 0
        for size in ROW_CLASSES:
            @pl.when(jnp.logical_and(nrows > lo, nrows <= size))
            def _(size=size):
                fn(size)
            lo = size

    @pl.when(jnp.logical_and(live, jnp.logical_and(u == 0, s == 0)))
    def _():
        x_copy(0).start()

    @pl.when(jnp.logical_and(live, s == 0))
    def _():
        x_copy(u).wait()

        def stage(r, c):
            r0 = pl.multiple_of(r * R_UNIT, R_UNIT)
            keep = (r0 + lax.broadcasted_iota(I32, (R_UNIT, 1), 0)) < nrows
            xbuf[pl.ds(r0, R_UNIT), :] = jnp.where(keep, xstage[pl.ds(r0, R_UNIT), :], 0.0).astype(BF16)
            return c
        lax.fori_loop(0, nunits, stage, 0)

        def zero(r, c):
            xbuf[pl.ds(pl.multiple_of(r * R_UNIT, R_UNIT), R_UNIT), :] = jnp.zeros((R_UNIT, D_MODEL), BF16)
            return c
        lax.fori_loop(nunits, R_BLK // R_UNIT, zero, 0)

    nxt = jnp.minimum(u + 1, N_BLK - 1)

    @pl.when(jnp.logical_and(s == 1, jnp.logical_and(u + 1 < N_BLK, rows_ref[nxt] > 0)))
    def _():
        x_copy(nxt).start()

    @pl.when(s < NC_UP)
    def _():
        chunk = jnp.minimum(s, NC_UP - 1)

        def up_proj(size):
            for h in range(F_UP // F_ACT):
                c0, c1 = h * F_ACT, (h + 1) * F_ACT
                w = jnp.concatenate([wg_ref[0, 0, :, c0:c1].astype(BF16),
                                     wu_ref[0, 0, :, c0:c1].astype(BF16)], axis=1)
                gu = _dot(xbuf[0:size, :], w)
                gate = jnp.minimum(gu[:, :F_ACT] + bg_ref[0, 0, :, c0:c1], SWIGLU_LIMIT)
                up = jnp.clip(gu[:, F_ACT:] + bu_ref[0, 0, :, c0:c1], -SWIGLU_LIMIT, SWIGLU_LIMIT)
                act = gate * _sigmoid(SWIGLU_ALPHA * gate) * (up + 1.0)
                actbuf[chunk * (F_UP // F_ACT) + h, 0:size, :] = act.astype(BF16)
        for_row_class(up_proj)

    @pl.when(s >= NC_UP)
    def _():
        def down_proj(size):
            acc = _dot(actbuf[0, 0:size, :], wd_ref[0, 0, 0:F_ACT, :].astype(BF16))
            for k in range(1, NC_ACT):
                acc = acc + _dot(actbuf[k, 0:size, :],
                                 wd_ref[0, 0, k * F_ACT:(k + 1) * F_ACT, :].astype(BF16))
            y_ref[0:size, :] = acc + bd_ref[0, 0]
            if size < R_BLK:
                y_ref[size:R_BLK, :] = jnp.zeros((R_BLK - size, F_DOWN), F32)
        for_row_class(down_proj)


def _experts(meta, xs, w_up, b_up4, w_down, b_down4, layer):
    eid, rows, valid, ueff = meta

    def up_chunk(u, s, valid):
        return jnp.where(valid[u] == 1, jnp.minimum(s, NC_UP - 1), NC_UP - 1)

    def down_chunk(u, s, valid):
        return jnp.where(valid[u] == 1, jnp.maximum(s - NC_UP, 0), NC_DOWN - 1)

    grid_spec = pltpu.PrefetchScalarGridSpec(
        num_scalar_prefetch=4,
        grid=(N_BLK, NC_UP + NC_DOWN),
        in_specs=[
            pl.BlockSpec(memory_space=pl.ANY),
            pl.BlockSpec((1, 1, D_MODEL, F_UP),
                         lambda u, s, e, r, v, f: (layer, e[u], 0, up_chunk(u, s, v))),
            pl.BlockSpec((1, 1, D_MODEL, F_UP),
                         lambda u, s, e, r, v, f: (layer, e[u], 0, NC_UP + up_chunk(u, s, v))),
            pl.BlockSpec((1, 1, 1, F_UP),
                         lambda u, s, e, r, v, f: (layer, e[u], 0, up_chunk(u, s, v))),
            pl.BlockSpec((1, 1, 1, F_UP),
                         lambda u, s, e, r, v, f: (layer, e[u], 0, NC_UP + up_chunk(u, s, v))),
            pl.BlockSpec((1, 1, D_FF, F_DOWN),
                         lambda u, s, e, r, v, f: (layer, e[u], 0, down_chunk(u, s, v))),
            pl.BlockSpec((1, 1, 1, F_DOWN),
                         lambda u, s, e, r, v, f: (layer, e[u], 0, down_chunk(u, s, v))),
        ],
        out_specs=pl.BlockSpec((R_BLK, F_DOWN), lambda u, s, e, r, v, f: (f[u], down_chunk(u, s, v))),
        scratch_shapes=[
            pltpu.VMEM((R_BLK, D_MODEL), F32),
            pltpu.VMEM((R_BLK, D_MODEL), BF16),
            pltpu.VMEM((NC_ACT, R_BLK, F_ACT), BF16),
            pltpu.SemaphoreType.DMA(()),
        ],
    )
    return pl.pallas_call(
        _experts_kernel,
        grid_spec=grid_spec,
        out_shape=jax.ShapeDtypeStruct((N_BLK * R_BLK, D_MODEL), F32),
        compiler_params=_cparams("arbitrary", "arbitrary", vmem=VMEM_LIMIT_EXPERTS),
        name="moe_experts",
    )(eid, rows, valid, ueff, xs, w_up, w_up, b_up4, b_up4, w_down, b_down4)


def _combine_kernel(pos_hbm, y_hbm, w_ref, x1_ref, gate2_ref, gpost2_ref, o_ref,
                    pos_smem, gbuf, sem_pos, sem_rows, *, tile_off, sets):
    i = pl.program_id(0)
    n = pl.num_programs(0)
    slot = i % 2
    n_src = sets * TOP_K

    def pos_copy(step, sl):
        return pltpu.make_async_copy(pos_hbm.at[pl.ds(step * sets + tile_off, sets)],
                                     pos_smem.at[sl], sem_pos.at[sl])

    def issue_rows(sl):
        def body(q, c):
            for dt in range(4):
                t = q * 4 + dt
                for k in range(n_src):
                    src = pos_smem[sl, k // TOP_K, (k % TOP_K) * TOK_TILE + t]
                    pltpu.make_async_copy(y_hbm.at[pl.ds(src, 1)], gbuf.at[sl, k, pl.ds(t, 1)],
                                          sem_rows.at[sl]).start(priority=k % 2)
            return c
        lax.fori_loop(0, TOK_TILE // 4, body, 0)

    @pl.when(i == 0)
    def _():
        pos_copy(0, 0).start()
        pos_copy(0, 0).wait()
        issue_rows(0)

        @pl.when(n > 1)
        def _():
            pos_copy(1, 1).start()

    @pl.when(i + 1 < n)
    def _():
        pos_copy(i + 1, 1 - slot).wait()
        issue_rows(1 - slot)

    @pl.when(i + 2 < n)
    def _():
        pos_copy(i + 2, slot).start()

    for k in range(n_src):
        pltpu.make_async_copy(y_hbm.at[pl.ds(0, TOK_TILE)], gbuf.at[slot, k], sem_rows.at[slot]).wait()

    w = w_ref[...]
    m = w[:, 0:1] * gbuf[slot, 0]
    for k in range(1, n_src):
        m = m + w[:, k:k + 1] * gbuf[slot, k]
    o_ref[...] = x1_ref[...] + gate2_ref[0] * (_rms(m) * gpost2_ref[...])


def _combine(pos_tiles, y_sorted, wgt_t, x1, mod3, g_post2, rows_per_mod, tile_off):
    m = x1.shape[0]
    r = mod3.shape[1]
    sets = wgt_t.shape[1] // TOP_K
    mod_row = lambda i: i // (rows_per_mod // TOK_TILE)
    return pl.pallas_call(
        functools.partial(_combine_kernel, tile_off=tile_off, sets=sets),
        grid=(m // TOK_TILE,),
        in_specs=[
            pl.BlockSpec(memory_space=pl.ANY),
            pl.BlockSpec(memory_space=pl.ANY),
            pl.BlockSpec((TOK_TILE, sets * TOP_K), lambda i: (i, 0)),
            pl.BlockSpec((TOK_TILE, D_MODEL), lambda i: (i, 0)),
            pl.BlockSpec((1, r, D_MODEL), lambda i: (mod_row(i), 0, 5)),
            pl.BlockSpec((1, D_MODEL), lambda i: (0, 0)),
        ],
        out_specs=pl.BlockSpec((TOK_TILE, D_MODEL), lambda i: (i, 0)),
        out_shape=jax.ShapeDtypeStruct((m, D_MODEL), F32),
        scratch_shapes=[
            pltpu.SMEM((2, sets, TOP_K * TOK_TILE), I32),
            pltpu.VMEM((2, sets * TOP_K, TOK_TILE, D_MODEL), F32),
            pltpu.SemaphoreType.DMA((2,)),
            pltpu.SemaphoreType.DMA((2,)),
        ],
        compiler_params=_cparams("arbitrary"),
        name="moe_combine",
    )(pos_tiles, y_sorted, wgt_t, x1, mod3, g_post2)


def _route_tables(counts, idx_all, rank_all):
    nblk = (counts + (R_BLK - 1)) // R_BLK
    cum = jnp.cumsum(nblk)
    base = cum - nblk
    total = cum[-1]
    u = jnp.arange(N_BLK, dtype=I32)
    valid = u < total
    ueff = jnp.minimum(u, jnp.maximum(total - 1, 0))
    eid = jnp.minimum(jnp.sum((cum[None, :] <= ueff[:, None]).astype(I32), axis=1), N_EXPERTS - 1)
    onehot_u = (eid[:, None] == jnp.arange(N_EXPERTS, dtype=I32)[None, :]).astype(I32)
    cnt_u = jnp.sum(onehot_u * counts[None, :], axis=1)
    base_u = jnp.sum(onehot_u * base[None, :], axis=1)
    rows = jnp.where(valid, jnp.clip(cnt_u - (ueff - base_u) * R_BLK, 0, R_BLK), 0)
    e_iota = jnp.arange(N_EXPERTS, dtype=I32)[None, None, :]
    base_tok = jnp.sum(jnp.where(idx_all[:, :, None] == e_iota, base[None, None, :], 0), axis=-1)
    pos = base_tok * R_BLK + rank_all
    n_tiles = idx_all.shape[1] // TOK_TILE
    pos_tiles = pos.reshape(TOP_K, n_tiles, TOK_TILE).transpose(1, 0, 2)
    pos_tiles = pos_tiles.reshape(n_tiles, TOP_K * TOK_TILE)
    return (eid.astype(I32), rows.astype(I32), valid.astype(I32), ueff.astype(I32)), pos_tiles.astype(I32)


def _layer_params(l, w):
    row = lambda a: a[l][None, :]
    return dict(
        g_pre1=row(w["g_pre1"]), g_post1=row(w["g_post1"]),
        g_pre2=row(w["g_pre2"]), g_post2=row(w["g_post2"]),
        w_gates_b=w["w_in"][l][:, N_REST:].astype(BF16),
        w_rest_b=w["w_in"][l][:, :N_REST].astype(BF16),
        w_pool_b=w["w_pool"][l].astype(BF16), b_pool=row(w["b_pool"]), pool_scale=row(w["pool_scale"]),
        ln_v_g=row(w["ln_v_g"]), ln_v_b=row(w["ln_v_b"]),
        w_s=w["w_s"][l], bs_full=jnp.repeat(w["b_s"][l].T, CHUNK, axis=1),
        ws0=jnp.repeat(w["w_s"][l][:, 0, 0], CHUNK)[None, :],
        bs0=jnp.repeat(w["b_s"][l][:, 0], CHUNK)[None, :],
        w_dw=w["w_dw"][l], b_dw=row(w["b_dw"]), ln_c_g=row(w["ln_c_g"]), ln_c_b=row(w["ln_c_b"]),
        w_pa_b=w["w_pa"][l].astype(BF16), w_pb_b=w["w_pb"][l].astype(BF16),
        w_pc_b=w["w_pc"][l].astype(BF16), w_o_b=w["w_o"][l].astype(BF16),
        w_rt=w["w_router"][l].T, w_rt_b=w["w_router"][l].T.astype(BF16),
        b_rt=w["b_router"][l][:, None],
    )


def kernel(x_prompt, x_sample, c_prompt, c_sample, state_pool, state_conv, w_ada, b_ada, g_pre1, g_post1, g_pre2, g_post2, w_in, b_in, w_pool, b_pool, pool_scale, ln_v_g, ln_v_b, w_s, b_s, w_dw, b_dw, ln_c_g, ln_c_b, w_pa, w_pb, w_pc, w_o, w_router, b_router, w_up, b_up, w_down, b_down):
    weights = dict(g_pre1=g_pre1, g_post1=g_post1, g_pre2=g_pre2, g_post2=g_post2, w_in=w_in,
                   w_pool=w_pool, b_pool=b_pool, pool_scale=pool_scale, ln_v_g=ln_v_g,
                   ln_v_b=ln_v_b, w_s=w_s, b_s=b_s, w_dw=w_dw, b_dw=b_dw, ln_c_g=ln_c_g,
                   ln_c_b=ln_c_b, w_pa=w_pa, w_pb=w_pb, w_pc=w_pc, w_o=w_o, w_router=w_router,
                   b_router=b_router)
    xp = x_prompt.reshape(N_PROMPT, D_MODEL)
    xs = x_sample.reshape(DEC_BATCH, D_MODEL)
    c_all = jnp.concatenate(
        [c_sample, c_prompt, jnp.zeros((N_MOD_PAD - N_MOD, D_MODEL), F32)], axis=0)
    b_ada3 = b_ada[:, None, :]
    b_in3 = b_in[:, None, :]
    b_gates3 = b_in[:, None, N_REST:]
    b_up4 = b_up[:, :, None, :]
    b_down4 = b_down[:, :, None, :]
    zero_cnt = jnp.zeros((N_EXPERTS, LANES), F32)

    pool_p, pool_s, conv_p, conv_s, v_s = [], [], [], [], []
    for l in range(DEPTH):
        p = _layer_params(l, weights)
        mod = _ada(c_all, w_ada, b_ada3, l)
        mod_s = mod[:DEC_BATCH][None]
        mod_p = mod[DEC_BATCH:N_MOD][:, None, :]

        gate_args = (p["g_pre1"], p["w_gates_b"], b_gates3, l, True)
        rest_args = (p["g_pre1"], p["w_rest_b"], b_in3, l, False)
        gates_p = _inproj(xp, mod_p, *gate_args, TM_IN, SEQ)
        zr_p = _inproj(xp, mod_p, *rest_args, TM_IN, SEQ)
        z_s = _inproj_sample(xs, mod_s, p["g_pre1"], w_in, b_in3, l)
        pa_p, sb_p, sc_p, npool, nconv = _mixer_prompt(zr_p, p)
        spool_t = jnp.transpose(state_pool[l], (1, 0, 2))
        sconv_t = jnp.transpose(state_conv[l], (1, 0, 2))
        pa_s, sb_s, sc_s, glu_s, vn_s = _mixer_sample(z_s, spool_t, sconv_t, w_pool, p, l)

        x1_p, hp_p, idx_p, wgt_p, rank_p, cnt_p = _merge(
            pa_p, sb_p, sc_p, gates_p, xp, mod_p, p, zero_cnt, TM_MERGE, SEQ)
        x1_s, hp_s, idx_s, wgt_s, rank_s, cnt_s = _merge_sample(
            pa_s, sb_s, sc_s, z_s, xs, mod_s, p, w_pa, w_pb, w_pc, w_o, cnt_p, l)
        idx_parts, rank_parts = [idx_p, idx_s], [rank_p, rank_s]
        wgt_s_t, cnt_last = wgt_s.T, cnt_s

        if l == 0:
            n = DEC_BATCH
            gates_m = _inproj(xs, mod_s, *gate_args, n, n)
            zr_m = _inproj(xs, mod_s, *rest_args, n, n)
            z_m = jnp.concatenate([zr_m[:, N_REST - W_A:], zr_m[:, :N_REST - W_A]], axis=1)
            pa_m, sb_m, sc_m, _, _ = _mixer_sample(z_m, spool_t, sconv_t, w_pool, p, l,
                                                   precise=False)
            _, _, idx_m, wgt_m, rank_m, cnt_last = _merge(
                pa_m.astype(BF16), sb_m.astype(BF16), sc_m.astype(BF16), gates_m, xs, mod_s, p,
                cnt_s, n, n)
            idx_parts.append(idx_m)
            rank_parts.append(rank_m)
            wgt_s_t = 0.5 * jnp.concatenate([wgt_s.T, wgt_m.T], axis=1)

        counts = cnt_last[:, 0].astype(I32)
        idx_all = jnp.concatenate(idx_parts, axis=1)
        rank_all = jnp.concatenate(rank_parts, axis=1)
        meta, pos_tiles = _route_tables(counts, idx_all, rank_all)
        x_sorted = _dispatch(pos_tiles, hp_p, hp_s)
        y_sorted = _experts(meta, x_sorted, w_up, b_up4, w_down, b_down4, l)
        xp = _combine(pos_tiles, y_sorted, wgt_p.T, x1_p, mod_p, p["g_post2"], SEQ, 0)
        xs = _combine(pos_tiles, y_sorted, wgt_s_t, x1_s, mod_s, p["g_post2"], DEC_BATCH,
                      N_PROMPT // TOK_TILE)

        pool_p.append(npool[:, POOL_HALO - POOL_BUF:])
        conv_p.append(nconv[:, CONV_HALO - CONV_BUF:])
        pool_s.append(jnp.concatenate([state_pool[l][:, 1:], z_s[:, None, :W_A]], axis=1))
        conv_s.append(jnp.concatenate([state_conv[l][:, 1:], glu_s[:, None, :]], axis=1))
        v_s.append(vn_s[:, None, :])

    return (xp.reshape(BATCH, SEQ, D_MODEL), xs.reshape(DEC_BATCH, 1, D_MODEL),
            jnp.stack(pool_p), jnp.stack(pool_s), jnp.stack(conv_p), jnp.stack(conv_s),
            jnp.stack(v_s))
```

```python
import functools

import jax
import jax.numpy as jnp
from jax import lax
from jax.experimental import pallas as pl
from jax.experimental.pallas import tpu as pltpu

F32 = jnp.float32
BF16 = jnp.bfloat16
I32 = jnp.int32

D_MODEL = 2048
BATCH = 4
SEQ = 2048
DEPTH = 2
DEC_BATCH = 128
PAST_LEN = 16384
W_A = 512
POOL_WINDOWS = (2, 4, 8, 16)
POOL_GROUP = 128
POOL_BUF = 15
W_B = 768
CHUNK = 128
H_B = 6
W_C = 768
CONV_WIDTH = 31
CONV_BUF = 30
N_GATES = 3 * D_MODEL
N_REST = W_A + 2 * W_B + 2 * W_C
N_IN = N_REST + N_GATES
N_EXPERTS = 32
TOP_K = 4
D_FF = D_MODEL
SWIGLU_LIMIT = 7.0
SWIGLU_ALPHA = 1.702
EPS = 1e-6

N_PROMPT = BATCH * SEQ
N_TOK = N_PROMPT + 2 * DEC_BATCH
N_MOD = DEC_BATCH + BATCH
N_MOD_PAD = 136

LANES = 128
SUBLANES = 8

VMEM_LIMIT = 56 * 1024 * 1024
VMEM_LIMIT_EXPERTS = 61 * 1024 * 1024

TN_ADA = 1024
TM_IN = 1024
TN_IN = 512
TN_GATES = 1024
TT_MIX = 256
POOL_HALO = 16
CONV_HALO = 32
TM_MERGE = 256
TN_MERGE_S = 512
TOK_TILE = 128
R_BLK = 1536
R_UNIT = 128
ROW_CLASSES = (896, 1152, R_BLK)
F_UP = 512
F_ACT = 256
F_DOWN = 512
NC_UP = D_FF // F_UP
NC_ACT = D_FF // F_ACT
NC_DOWN = D_MODEL // F_DOWN
N_BLK = N_EXPERTS + (N_TOK * TOP_K) // R_BLK

HI = lax.Precision.HIGHEST


def _cparams(*sem, vmem=VMEM_LIMIT):
    return pltpu.CompilerParams(dimension_semantics=sem, vmem_limit_bytes=vmem)


def _sigmoid(x):
    return 1.0 / (1.0 + jnp.exp(-x))


def _gelu_tanh(x):
    c = 0.7978845608028654
    return x * (0.5 * (1.0 + jnp.tanh(c * (x + 0.044715 * (x * x * x)))))


def _rms(x):
    return x * lax.rsqrt(jnp.mean(x * x, axis=-1, keepdims=True) + EPS)


def _layernorm(x, g, b):
    mu = jnp.mean(x, axis=-1, keepdims=True)
    xc = x - mu
    return xc * lax.rsqrt(jnp.mean(xc * xc, axis=-1, keepdims=True) + EPS) * g + b


def _dot(a, b):
    return jnp.dot(a, b, preferred_element_type=F32)


def _dot32(a, b):
    return jnp.dot(a, b, preferred_element_type=F32, precision=HI)


def _ada_kernel(c_ref, w_ref, b_ref, o_ref):
    c = c_ref[...]
    s = (c * _sigmoid(c)).astype(BF16)
    o_ref[...] = _dot(s, w_ref[0].astype(BF16)) + b_ref[0]


def _ada(c_all, w_ada, b_ada3, layer):
    n = 6 * D_MODEL
    return pl.pallas_call(
        _ada_kernel,
        grid=(n // TN_ADA,),
        in_specs=[
            pl.BlockSpec((N_MOD_PAD, D_MODEL), lambda j: (0, 0)),
            pl.BlockSpec((1, D_MODEL, TN_ADA), lambda j: (layer, 0, j)),
            pl.BlockSpec((1, 1, TN_ADA), lambda j: (layer, 0, j)),
        ],
        out_specs=pl.BlockSpec((N_MOD_PAD, TN_ADA), lambda j: (0, j)),
        out_shape=jax.ShapeDtypeStruct((N_MOD_PAD, n), F32),
        compiler_params=_cparams("arbitrary"),
        name="ada",
    )(c_all, w_ada, b_ada3)


N_REST_TILES = N_REST // TN_IN


def _rest_col_tile(j):
    a_tiles = W_A // TN_IN
    return jnp.where(j < N_REST_TILES - a_tiles, j + a_tiles, j - (N_REST_TILES - a_tiles))


def _inproj_kernel(x_ref, sc_ref, sh_ref, g_ref, w_ref, b_ref, o_ref, h_scr, *, gates):
    @pl.when(pl.program_id(1) == 0)
    def _():
        y = _rms(x_ref[...]) * g_ref[...]
        h_scr[...] = (y * (1.0 + sc_ref[0]) + sh_ref[0]).astype(BF16)

    z = _dot(h_scr[...], w_ref[...]) + b_ref[0]
    o_ref[...] = _sigmoid(z).astype(BF16) if gates else z


def _inproj(x, mod3, g_pre, w_b, b3, layer, gates, tm, rows_per_mod):
    m = x.shape[0]
    r = mod3.shape[1]
    mod_row = lambda i: i // (rows_per_mod // tm)
    col = (lambda j: j) if gates else _rest_col_tile
    n_out = N_GATES if gates else N_REST
    tn = TN_GATES if gates else TN_IN
    return pl.pallas_call(
        functools.partial(_inproj_kernel, gates=gates),
        grid=(m // tm, n_out // tn),
        in_specs=[
            pl.BlockSpec((tm, D_MODEL), lambda i, j: (i, 0)),
            pl.BlockSpec((1, r, D_MODEL), lambda i, j: (mod_row(i), 0, 1)),
            pl.BlockSpec((1, r, D_MODEL), lambda i, j: (mod_row(i), 0, 0)),
            pl.BlockSpec((1, D_MODEL), lambda i, j: (0, 0)),
            pl.BlockSpec((D_MODEL, tn), lambda i, j: (0, col(j))),
            pl.BlockSpec((1, 1, tn), lambda i, j: (layer, 0, col(j))),
        ],
        out_specs=pl.BlockSpec((tm, tn), lambda i, j: (i, j)),
        out_shape=jax.ShapeDtypeStruct((m, n_out), BF16 if gates else F32),
        scratch_shapes=[pltpu.VMEM((tm, D_MODEL), BF16)],
        compiler_params=_cparams("arbitrary", "arbitrary"),
        name="inproj_gates" if gates else "inproj_rest",
    )(x, mod3, mod3, g_pre, w_b, b3)


def _inproj_sample_kernel(x_ref, sc_ref, sh_ref, g_ref, w_ref, b_ref, z_ref, h_scr):
    @pl.when(pl.program_id(0) == 0)
    def _():
        y = _rms(x_ref[...]) * g_ref[...]
        h_scr[...] = y * (1.0 + sc_ref[0]) + sh_ref[0]

    z_ref[...] = _dot32(h_scr[...], w_ref[0]) + b_ref[0]


def _inproj_sample(x, mod3, g_pre, w_in, b_in3, layer):
    n = DEC_BATCH
    return pl.pallas_call(
        _inproj_sample_kernel,
        grid=(N_IN // TN_IN,),
        in_specs=[
            pl.BlockSpec((n, D_MODEL), lambda j: (0, 0)),
            pl.BlockSpec((1, n, D_MODEL), lambda j: (0, 0, 1)),
            pl.BlockSpec((1, n, D_MODEL), lambda j: (0, 0, 0)),
            pl.BlockSpec((1, D_MODEL), lambda j: (0, 0)),
            pl.BlockSpec((1, D_MODEL, TN_IN), lambda j: (layer, 0, j)),
            pl.BlockSpec((1, 1, TN_IN), lambda j: (layer, 0, j)),
        ],
        out_specs=pl.BlockSpec((n, TN_IN), lambda j: (0, j)),
        out_shape=jax.ShapeDtypeStruct((n, N_IN), F32),
        scratch_shapes=[pltpu.VMEM((n, D_MODEL), F32)],
        compiler_params=_cparams("arbitrary"),
        name="inproj_sample",
    )(x, mod3, mod3, g_pre, w_in, b_in3)


def _mixer_prompt_kernel(uv_ref, cg_ref, a_ref, wpool_ref, bpool_ref, pscale_ref,
                         lvg_ref, lvb_ref, ws_ref, bs_ref, wdw_ref, bdw_ref, lcg_ref, lcb_ref,
                         pa_ref, sb_ref, sc_ref, npool_ref, nconv_ref,
                         aext, gext, cv_scr):
    t = pl.program_id(1)
    tt = TT_MIX

    @pl.when(t == 0)
    def _():
        aext[0:POOL_HALO, :] = jnp.zeros((POOL_HALO, W_A), F32)
        gext[0:CONV_HALO, :] = jnp.zeros((CONV_HALO, W_C), F32)

    a = a_ref[...]
    aext[POOL_HALO:POOL_HALO + tt, :] = a
    pos = t * tt + lax.broadcasted_iota(I32, (tt, 1), 0)
    for g, w in enumerate(POOL_WINDOWS):
        c0, c1 = g * POOL_GROUP, (g + 1) * POOL_GROUP
        cur = a[:, c0:c1]
        s = cur
        for j in range(1, w):
            s = s + aext[POOL_HALO - j:POOL_HALO - j + tt, c0:c1]
        cnt = jnp.minimum(w, pos + 1).astype(F32)
        d = (s / cnt - cur).astype(BF16)
        og = _dot(d, wpool_ref[g])
        pa_ref[:, c0:c1] = ((og + bpool_ref[:, c0:c1]) * pscale_ref[:, c0:c1]).astype(BF16)
    tail_a = aext[tt:tt + POOL_HALO, :]
    npool_ref[0] = tail_a
    aext[0:POOL_HALO, :] = tail_a

    uv = _gelu_tanh(uv_ref[...])
    u = uv[:, :W_B]
    vn = _layernorm(uv[:, W_B:], lvg_ref[...], lvb_ref[...]).astype(BF16)
    ri = lax.broadcasted_iota(I32, (CHUNK, CHUNK), 0)
    ci = lax.broadcasted_iota(I32, (CHUNK, CHUNK), 1)
    for h in range(H_B):
        h0, h1 = h * CHUNK, (h + 1) * CHUNK
        wsm = jnp.where(ri >= ci, ws_ref[h], 0.0).astype(BF16)
        for c in range(tt // CHUNK):
            r0, r1 = c * CHUNK, (c + 1) * CHUNK
            s = _dot(wsm, vn[r0:r1, h0:h1]) + bs_ref[:, h0:h1]
            sb_ref[r0:r1, h0:h1] = (u[r0:r1, h0:h1] * s).astype(BF16)

    cg = cg_ref[...]
    gext[CONV_HALO:CONV_HALO + tt, :] = cg[:, :W_C] * _sigmoid(cg[:, W_C:])
    rb = 64
    for cb in range(W_C // LANES):
        c0, c1 = cb * LANES, (cb + 1) * LANES
        for r in range(tt // rb):
            base = CONV_HALO - CONV_BUF + r * rb
            acc = gext[base:base + rb, c0:c1] * wdw_ref[0:1, c0:c1]
            for k in range(1, CONV_WIDTH):
                acc = acc + gext[base + k:base + k + rb, c0:c1] * wdw_ref[k:k + 1, c0:c1]
            cv_scr[r * rb:(r + 1) * rb, c0:c1] = acc
    cv = _layernorm(cv_scr[...] + bdw_ref[...], lcg_ref[...], lcb_ref[...])
    sc_ref[...] = (cv * _sigmoid(cv)).astype(BF16)
    tail_g = gext[tt:tt + CONV_HALO, :]
    nconv_ref[0] = tail_g
    gext[0:CONV_HALO, :] = tail_g


def _mixer_prompt(zrest, p):
    nt = SEQ // TT_MIX
    row = lambda b, t: b * nt + t
    vec = lambda n: pl.BlockSpec((1, n), lambda b, t: (0, 0))
    return pl.pallas_call(
        _mixer_prompt_kernel,
        grid=(BATCH, nt),
        in_specs=[
            pl.BlockSpec((TT_MIX, 2 * W_B), lambda b, t: (row(b, t), 0)),
            pl.BlockSpec((TT_MIX, 2 * W_C), lambda b, t: (row(b, t), 1)),
            pl.BlockSpec((TT_MIX, W_A), lambda b, t: (row(b, t), (2 * W_B + 2 * W_C) // W_A)),
            pl.BlockSpec((len(POOL_WINDOWS), POOL_GROUP, POOL_GROUP), lambda b, t: (0, 0, 0)),
            vec(W_A), vec(W_A), vec(W_B), vec(W_B),
            pl.BlockSpec((H_B, CHUNK, CHUNK), lambda b, t: (0, 0, 0)),
            pl.BlockSpec((CHUNK, W_B), lambda b, t: (0, 0)),
            pl.BlockSpec((CONV_WIDTH, W_C), lambda b, t: (0, 0)),
            vec(W_C), vec(W_C), vec(W_C),
        ],
        out_specs=[
            pl.BlockSpec((TT_MIX, W_A), lambda b, t: (row(b, t), 0)),
            pl.BlockSpec((TT_MIX, W_B), lambda b, t: (row(b, t), 0)),
            pl.BlockSpec((TT_MIX, W_C), lambda b, t: (row(b, t), 0)),
            pl.BlockSpec((1, POOL_HALO, W_A), lambda b, t: (b, 0, 0)),
            pl.BlockSpec((1, CONV_HALO, W_C), lambda b, t: (b, 0, 0)),
        ],
        out_shape=[
            jax.ShapeDtypeStruct((N_PROMPT, W_A), BF16),
            jax.ShapeDtypeStruct((N_PROMPT, W_B), BF16),
            jax.ShapeDtypeStruct((N_PROMPT, W_C), BF16),
            jax.ShapeDtypeStruct((BATCH, POOL_HALO, W_A), F32),
            jax.ShapeDtypeStruct((BATCH, CONV_HALO, W_C), F32),
        ],
        scratch_shapes=[
            pltpu.VMEM((TT_MIX + POOL_HALO, W_A), F32),
            pltpu.VMEM((TT_MIX + CONV_HALO, W_C), F32),
            pltpu.VMEM((TT_MIX, W_C), F32),
        ],
        compiler_params=_cparams("arbitrary", "arbitrary"),
        name="mixer_prompt",
    )(zrest, zrest, zrest, p["w_pool_b"], p["b_pool"], p["pool_scale"], p["ln_v_g"], p["ln_v_b"],
      p["w_s"], p["bs_full"], p["w_dw"], p["b_dw"], p["ln_c_g"], p["ln_c_b"])


def _mixer_sample_kernel(z_ref, spool_ref, sconv_ref, wpool_ref, bpool_ref,
                         pscale_ref, lvg_ref, lvb_ref, ws0_ref, bs0_ref, wdw_ref, bdw_ref,
                         lcg_ref, lcb_ref, pa_ref, sb_ref, sc_ref, glu_ref, v_ref, *, precise):
    rnd = (lambda t: t) if precise else (lambda t: t.astype(BF16).astype(F32))
    a = z_ref[:, 0:W_A]
    for g, w in enumerate(POOL_WINDOWS):
        c0, c1 = g * POOL_GROUP, (g + 1) * POOL_GROUP
        cur = a[:, c0:c1]
        s = cur
        for j in range(1, w):
            s = s + spool_ref[POOL_BUF - j, :, c0:c1]
        cnt = float(min(w, PAST_LEN + 1))
        og = _dot32(rnd(s / cnt - cur), rnd(wpool_ref[0, g]))
        pa_ref[:, c0:c1] = (og + bpool_ref[:, c0:c1]) * pscale_ref[:, c0:c1]

    u = _gelu_tanh(z_ref[:, W_A:W_A + W_B])
    vn = _layernorm(_gelu_tanh(z_ref[:, W_A + W_B:W_A + 2 * W_B]), lvg_ref[...], lvb_ref[...])
    v_ref[...] = vn
    sb_ref[...] = u * (rnd(ws0_ref[...]) * rnd(vn) + bs0_ref[...])

    c0 = W_A + 2 * W_B
    glu = z_ref[:, c0:c0 + W_C] * _sigmoid(z_ref[:, c0 + W_C:c0 + 2 * W_C])
    glu_ref[...] = glu
    acc = glu * wdw_ref[CONV_BUF:CONV_BUF + 1, :]
    for k in range(CONV_BUF):
        acc = acc + sconv_ref[k] * wdw_ref[k:k + 1, :]
    cv = _layernorm(acc + bdw_ref[...], lcg_ref[...], lcb_ref[...])
    sc_ref[...] = cv * _sigmoid(cv)


def _mixer_sample(z, spool_t, sconv_t, w_pool, p, layer, precise=True):
    n = DEC_BATCH
    vec = lambda w: pl.BlockSpec((1, w), lambda i: (0, 0))
    out = lambda w: pl.BlockSpec((n, w), lambda i: (0, 0))
    return pl.pallas_call(
        functools.partial(_mixer_sample_kernel, precise=precise),
        grid=(1,),
        in_specs=[
            pl.BlockSpec((n, N_REST), lambda i: (0, 0)),
            pl.BlockSpec((POOL_BUF, n, W_A), lambda i: (0, 0, 0)),
            pl.BlockSpec((CONV_BUF, n, W_C), lambda i: (0, 0, 0)),
            pl.BlockSpec((1, len(POOL_WINDOWS), POOL_GROUP, POOL_GROUP), lambda i: (layer, 0, 0, 0)),
            vec(W_A), vec(W_A), vec(W_B), vec(W_B), vec(W_B), vec(W_B),
            pl.BlockSpec((CONV_WIDTH, W_C), lambda i: (0, 0)),
            vec(W_C), vec(W_C), vec(W_C),
        ],
        out_specs=[out(W_A), out(W_B), out(W_C), out(W_C), out(W_B)],
        out_shape=[
            jax.ShapeDtypeStruct((n, W_A), F32),
            jax.ShapeDtypeStruct((n, W_B), F32),
            jax.ShapeDtypeStruct((n, W_C), F32),
            jax.ShapeDtypeStruct((n, W_C), F32),
            jax.ShapeDtypeStruct((n, W_B), F32),
        ],
        compiler_params=_cparams("arbitrary"),
        name="mixer_sample",
    )(z, spool_t, sconv_t, w_pool, p["b_pool"], p["pool_scale"],
      p["ln_v_g"], p["ln_v_b"], p["ws0"], p["bs0"], p["w_dw"], p["b_dw"], p["ln_c_g"], p["ln_c_b"])


def _route(logits, idx_ref, wgt_ref, rank_ref, cnt_ref, cnt_scr):
    tm = logits.shape[1]
    eidx = lax.broadcasted_iota(I32, (N_EXPERTS, tm), 0).astype(F32)
    work = logits
    vals, onehots = [], []
    for k in range(TOP_K):
        mx = jnp.max(work, axis=0, keepdims=True)
        ik = jnp.min(jnp.where(work == mx, eidx, float(N_EXPERTS)), axis=0, keepdims=True)
        oh = eidx == ik
        vals.append(mx)
        onehots.append(oh)
        idx_ref[k:k + 1, :] = ik.astype(I32)
        work = jnp.where(oh, -jnp.inf, work)
    ex = [jnp.exp(v - vals[0]) for v in vals]
    den = ex[0] + ex[1] + ex[2] + ex[3]
    for k in range(TOP_K):
        wgt_ref[k:k + 1, :] = ex[k] / den

    sel = jnp.zeros((N_EXPERTS, tm), F32)
    for oh in onehots:
        sel = sel + jnp.where(oh, 1.0, 0.0)
    ri = lax.broadcasted_iota(I32, (tm, tm), 0)
    ci = lax.broadcasted_iota(I32, (tm, tm), 1)
    before = jnp.where(ri < ci, 1.0, 0.0).astype(BF16)
    base = cnt_scr[:, 0:1] + _dot(sel.astype(BF16), before)
    for k in range(TOP_K):
        rank_ref[k:k + 1, :] = jnp.sum(jnp.where(onehots[k], base, 0.0), axis=0,
                                       keepdims=True).astype(I32)
    cnt_scr[...] = cnt_scr[...] + jnp.sum(sel, axis=1, keepdims=True)
    cnt_ref[...] = cnt_scr[...]


def _pack_rows(h2, hp_ref):
    hp_ref[...] = h2


def _merge_kernel(pa_ref, sb_ref, sc_ref, gates_ref, x_ref, gate1_ref, scale2_ref, shift2_ref,
                  gpost1_ref, gpre2_ref, wpa_ref, wpb_ref, wpc_ref, wo_ref, wrt_ref, brt_ref,
                  cnt_in_ref, x1_ref, hp_ref, idx_ref, wgt_ref, rank_ref, cnt_ref, cnt_scr):
    @pl.when(pl.program_id(0) == 0)
    def _():
        cnt_scr[...] = cnt_in_ref[...]

    g = gates_ref[...]
    m = g[:, :D_MODEL].astype(F32) * _dot(pa_ref[...], wpa_ref[...])
    m = m + g[:, D_MODEL:2 * D_MODEL].astype(F32) * _dot(sb_ref[...], wpb_ref[...])
    m = m + g[:, 2 * D_MODEL:].astype(F32) * _dot(sc_ref[...], wpc_ref[...])
    o = _dot(m.astype(BF16), wo_ref[...])
    x1 = x_ref[...] + gate1_ref[0] * (_rms(o) * gpost1_ref[...])
    x1_ref[...] = x1
    h2 = _rms(x1) * gpre2_ref[...] * (1.0 + scale2_ref[0]) + shift2_ref[0]
    _pack_rows(h2, hp_ref)
    logits = lax.dot_general(wrt_ref[...], h2.astype(BF16), (((1,), (1,)), ((), ())),
                             preferred_element_type=F32) + brt_ref[...]
    _route(logits, idx_ref, wgt_ref, rank_ref, cnt_ref, cnt_scr)


def _merge(pa, sb, sc, gates, x, mod3, p, cnt_in, tm, rows_per_mod):
    m = x.shape[0]
    r = mod3.shape[1]
    mod_row = lambda i: i // (rows_per_mod // tm)
    modspec = lambda k: pl.BlockSpec((1, r, D_MODEL), lambda i: (mod_row(i), 0, k))
    const = lambda shape: pl.BlockSpec(shape, lambda i: (0,) * len(shape),
                                       pipeline_mode=pl.Buffered(1))
    row = lambda w: pl.BlockSpec((tm, w), lambda i: (i, 0))
    tok = pl.BlockSpec((TOP_K, tm), lambda i: (0, i))
    return pl.pallas_call(
        _merge_kernel,
        grid=(m // tm,),
        in_specs=[
            row(W_A), row(W_B), row(W_C), row(N_GATES), row(D_MODEL),
            modspec(2), modspec(4), modspec(3),
            const((1, D_MODEL)), const((1, D_MODEL)),
            const((W_A, D_MODEL)), const((W_B, D_MODEL)), const((W_C, D_MODEL)),
            const((D_MODEL, D_MODEL)), const((N_EXPERTS, D_MODEL)), const((N_EXPERTS, 1)),
            const((N_EXPERTS, LANES)),
        ],
        out_specs=[
            row(D_MODEL),
            row(D_MODEL),
            tok, tok, tok,
            pl.BlockSpec((N_EXPERTS, LANES), lambda i: (0, 0)),
        ],
        out_shape=[
            jax.ShapeDtypeStruct((m, D_MODEL), F32),
            jax.ShapeDtypeStruct((m, D_MODEL), F32),
            jax.ShapeDtypeStruct((TOP_K, m), I32),
            jax.ShapeDtypeStruct((TOP_K, m), F32),
            jax.ShapeDtypeStruct((TOP_K, m), I32),
            jax.ShapeDtypeStruct((N_EXPERTS, LANES), F32),
        ],
        scratch_shapes=[pltpu.VMEM((N_EXPERTS, LANES), F32)],
        compiler_params=_cparams("arbitrary"),
        name="merge_router",
    )(pa, sb, sc, gates, x, mod3, mod3, mod3, p["g_post1"], p["g_pre2"],
      p["w_pa_b"], p["w_pb_b"], p["w_pc_b"], p["w_o_b"], p["w_rt_b"], p["b_rt"], cnt_in)


NT_MERGE_S = D_MODEL // TN_MERGE_S


def _merge_sample_kernel(pa_ref, sb_ref, sc_ref, ga_ref, gb_ref, gc_ref, x_ref,
                         gate1_ref, scale2_ref, shift2_ref, gpost1_ref, gpre2_ref,
                         wpa_ref, wpb_ref, wpc_ref, wo_ref, wrt_ref, brt_ref, cnt_in_ref,
                         x1_ref, hp_ref, idx_ref, wgt_ref, rank_ref, cnt_ref,
                         m_scr, o_scr, cnt_scr):
    s = pl.program_id(0)
    tn = TN_MERGE_S

    @pl.when(s < NT_MERGE_S)
    def _():
        m = _sigmoid(ga_ref[...]) * _dot32(pa_ref[...], wpa_ref[0])
        m = m + _sigmoid(gb_ref[...]) * _dot32(sb_ref[...], wpb_ref[0])
        m = m + _sigmoid(gc_ref[...]) * _dot32(sc_ref[...], wpc_ref[0])
        m_scr[jnp.minimum(s, NT_MERGE_S - 1)] = m

    @pl.when(jnp.logical_and(s >= NT_MERGE_S, s < 2 * NT_MERGE_S))
    def _():
        o = _dot32(m_scr[0], wo_ref[0, 0:tn, :])
        for k in range(1, NT_MERGE_S):
            o = o + _dot32(m_scr[k], wo_ref[0, k * tn:(k + 1) * tn, :])
        o_scr[jnp.clip(s - NT_MERGE_S, 0, NT_MERGE_S - 1)] = o

    @pl.when(s == 2 * NT_MERGE_S)
    def _():
        cnt_scr[...] = cnt_in_ref[...]
        o = jnp.concatenate([o_scr[k] for k in range(NT_MERGE_S)], axis=1)
        x1 = x_ref[...] + gate1_ref[0] * (_rms(o) * gpost1_ref[...])
        x1_ref[...] = x1
        h2 = _rms(x1) * gpre2_ref[...] * (1.0 + scale2_ref[0]) + shift2_ref[0]
        _pack_rows(h2, hp_ref)
        logits = lax.dot_general(wrt_ref[...], h2, (((1,), (1,)), ((), ())),
                                 preferred_element_type=F32, precision=HI) + brt_ref[...]
        _route(logits, idx_ref, wgt_ref, rank_ref, cnt_ref, cnt_scr)


def _merge_sample(pa, sb, sc, z, x, mod3, p, w_pa, w_pb, w_pc, w_o, cnt_in, layer):
    n = DEC_BATCH
    tn = TN_MERGE_S
    nt = NT_MERGE_S
    t1 = lambda s: jnp.minimum(s, nt - 1)
    t2 = lambda s: jnp.clip(s - nt, 0, nt - 1)
    full = lambda shape: pl.BlockSpec(shape, lambda s: (0,) * len(shape))
    modspec = lambda k: pl.BlockSpec((1, n, D_MODEL), lambda s: (0, 0, k))
    gate_cols = lambda b: pl.BlockSpec((n, tn), lambda s: (0, (N_REST + b * D_MODEL) // tn + t1(s)))
    tok = pl.BlockSpec((TOP_K, n), lambda s: (0, 0))
    return pl.pallas_call(
        _merge_sample_kernel,
        grid=(2 * nt + 1,),
        in_specs=[
            full((n, W_A)), full((n, W_B)), full((n, W_C)),
            gate_cols(0), gate_cols(1), gate_cols(2),
            full((n, D_MODEL)),
            modspec(2), modspec(4), modspec(3),
            full((1, D_MODEL)), full((1, D_MODEL)),
            pl.BlockSpec((1, W_A, tn), lambda s: (layer, 0, t1(s))),
            pl.BlockSpec((1, W_B, tn), lambda s: (layer, 0, t1(s))),
            pl.BlockSpec((1, W_C, tn), lambda s: (layer, 0, t1(s))),
            pl.BlockSpec((1, D_MODEL, tn), lambda s: (layer, 0, t2(s))),
            full((N_EXPERTS, D_MODEL)), full((N_EXPERTS, 1)), full((N_EXPERTS, LANES)),
        ],
        out_specs=[
            full((n, D_MODEL)),
            full((n, D_MODEL)),
            tok, tok, tok,
            full((N_EXPERTS, LANES)),
        ],
        out_shape=[
            jax.ShapeDtypeStruct((n, D_MODEL), F32),
            jax.ShapeDtypeStruct((n, D_MODEL), F32),
            jax.ShapeDtypeStruct((TOP_K, n), I32),
            jax.ShapeDtypeStruct((TOP_K, n), F32),
            jax.ShapeDtypeStruct((TOP_K, n), I32),
            jax.ShapeDtypeStruct((N_EXPERTS, LANES), F32),
        ],
        scratch_shapes=[
            pltpu.VMEM((nt, n, tn), F32),
            pltpu.VMEM((nt, n, tn), F32),
            pltpu.VMEM((N_EXPERTS, LANES), F32),
        ],
        compiler_params=_cparams("arbitrary"),
        name="merge_router_sample",
    )(pa, sb, sc, z, z, z, x, mod3, mod3, mod3, p["g_post1"], p["g_pre2"],
      w_pa, w_pb, w_pc, w_o, p["w_rt"], p["b_rt"], cnt_in)


def _dispatch_kernel(pos_hbm, hp_ref, hs_ref, xs_hbm, pos_smem, sem_pos, sem_rows):
    i = pl.program_id(0)
    cp = pltpu.make_async_copy(pos_hbm.at[pl.ds(i, 1)], pos_smem, sem_pos)
    cp.start()
    cp.wait()

    def scatter_rows(src_ref):
        for t in range(TOK_TILE):
            for k in range(TOP_K):
                dst = pos_smem[0, k * TOK_TILE + t]
                pltpu.make_async_copy(src_ref.at[pl.ds(t, 1)], xs_hbm.at[pl.ds(dst, 1)],
                                      sem_rows).start(priority=k % 2)
        for _ in range(TOP_K):
            pltpu.make_async_copy(src_ref, xs_hbm.at[pl.ds(0, TOK_TILE)], sem_rows).wait()

    @pl.when(i < N_PROMPT // TOK_TILE)
    def _():
        scatter_rows(hp_ref)

    @pl.when(i >= N_PROMPT // TOK_TILE)
    def _():
        scatter_rows(hs_ref)


def _dispatch(pos_tiles, hp_p, hp_s):
    n_prompt_tiles = N_PROMPT // TOK_TILE
    return pl.pallas_call(
        _dispatch_kernel,
        grid=(pos_tiles.shape[0],),
        in_specs=[
            pl.BlockSpec(memory_space=pl.ANY),
            pl.BlockSpec((TOK_TILE, D_MODEL), lambda i: (jnp.minimum(i, n_prompt_tiles - 1), 0)),
            pl.BlockSpec((TOK_TILE, D_MODEL), lambda i: (0, 0)),
        ],
        out_specs=pl.BlockSpec(memory_space=pl.ANY),
        out_shape=jax.ShapeDtypeStruct((N_BLK * R_BLK, D_MODEL), F32),
        scratch_shapes=[
            pltpu.SMEM((1, TOP_K * TOK_TILE), I32),
            pltpu.SemaphoreType.DMA(()),
            pltpu.SemaphoreType.DMA(()),
        ],
        compiler_params=_cparams("arbitrary"),
        name="moe_dispatch",
    )(pos_tiles, hp_p, hp_s)


def _experts_kernel(eid_ref, rows_ref, valid_ref, ueff_ref,
                    x_hbm, wg_ref, wu_ref, bg_ref, bu_ref, wd_ref, bd_ref, y_ref,
                    xstage, xbuf, actbuf, sem_x):
    del eid_ref, valid_ref, ueff_ref
    u = pl.program_id(0)
    s = pl.program_id(1)
    nrows = rows_ref[u]
    nunits = (nrows + (R_UNIT - 1)) // R_UNIT
    live = nrows > 0

    def x_copy(blk):
        return pltpu.make_async_copy(x_hbm.at[pl.ds(blk * R_BLK, R_BLK)], xstage, sem_x)

    def for_row_class(fn):
        lo = 0
        for size in ROW_CLASSES:
            @pl.when(jnp.logical_and(nrows > lo, nrows <= size))
            def _(size=size):
                fn(size)
            lo = size

    @pl.when(jnp.logical_and(live, jnp.logical_and(u == 0, s == 0)))
    def _():
        x_copy(0).start()

    @pl.when(jnp.logical_and(live, s == 0))
    def _():
        x_copy(u).wait()

        def stage(r, c):
            r0 = pl.multiple_of(r * R_UNIT, R_UNIT)
            keep = (r0 + lax.broadcasted_iota(I32, (R_UNIT, 1), 0)) < nrows
            xbuf[pl.ds(r0, R_UNIT), :] = jnp.where(keep, xstage[pl.ds(r0, R_UNIT), :], 0.0).astype(BF16)
            return c
        lax.fori_loop(0, nunits, stage, 0)

        def zero(r, c):
            xbuf[pl.ds(pl.multiple_of(r * R_UNIT, R_UNIT), R_UNIT), :] = jnp.zeros((R_UNIT, D_MODEL), BF16)
            return c
        lax.fori_loop(nunits, R_BLK // R_UNIT, zero, 0)

    nxt = jnp.minimum(u + 1, N_BLK - 1)

    @pl.when(jnp.logical_and(s == 1, jnp.logical_and(u + 1 < N_BLK, rows_ref[nxt] > 0)))
    def _():
        x_copy(nxt).start()

    @pl.when(s < NC_UP)
    def _():
        chunk = jnp.minimum(s, NC_UP - 1)

        def up_proj(size):
            for h in range(F_UP // F_ACT):
                c0, c1 = h * F_ACT, (h + 1) * F_ACT
                w = jnp.concatenate([wg_ref[0, 0, :, c0:c1].astype(BF16),
                                     wu_ref[0, 0, :, c0:c1].astype(BF16)], axis=1)
                gu = _dot(xbuf[0:size, :], w)
                gate = jnp.minimum(gu[:, :F_ACT] + bg_ref[0, 0, :, c0:c1], SWIGLU_LIMIT)
                up = jnp.clip(gu[:, F_ACT:] + bu_ref[0, 0, :, c0:c1], -SWIGLU_LIMIT, SWIGLU_LIMIT)
                act = gate * _sigmoid(SWIGLU_ALPHA * gate) * (up + 1.0)
                actbuf[chunk * (F_UP // F_ACT) + h, 0:size, :] = act.astype(BF16)
        for_row_class(up_proj)

    @pl.when(s >= NC_UP)
    def _():
        def down_proj(size):
            acc = _dot(actbuf[0, 0:size, :], wd_ref[0, 0, 0:F_ACT, :].astype(BF16))
            for k in range(1, NC_ACT):
                acc = acc + _dot(actbuf[k, 0:size, :],
                                 wd_ref[0, 0, k * F_ACT:(k + 1) * F_ACT, :].astype(BF16))
            y_ref[0:size, :] = acc + bd_ref[0, 0]
            if size < R_BLK:
                y_ref[size:R_BLK, :] = jnp.zeros((R_BLK - size, F_DOWN), F32)
        for_row_class(down_proj)


def _experts(meta, xs, w_up, b_up4, w_down, b_down4, layer):
    eid, rows, valid, ueff = meta

    def up_chunk(u, s, valid):
        return jnp.where(valid[u] == 1, jnp.minimum(s, NC_UP - 1), NC_UP - 1)

    def down_chunk(u, s, valid):
        return jnp.where(valid[u] == 1, jnp.maximum(s - NC_UP, 0), NC_DOWN - 1)

    grid_spec = pltpu.PrefetchScalarGridSpec(
        num_scalar_prefetch=4,
        grid=(N_BLK, NC_UP + NC_DOWN),
        in_specs=[
            pl.BlockSpec(memory_space=pl.ANY),
            pl.BlockSpec((1, 1, D_MODEL, F_UP),
                         lambda u, s, e, r, v, f: (layer, e[u], 0, up_chunk(u, s, v))),
            pl.BlockSpec((1, 1, D_MODEL, F_UP),
                         lambda u, s, e, r, v, f: (layer, e[u], 0, NC_UP + up_chunk(u, s, v))),
            pl.BlockSpec((1, 1, 1, F_UP),
                         lambda u, s, e, r, v, f: (layer, e[u], 0, up_chunk(u, s, v))),
            pl.BlockSpec((1, 1, 1, F_UP),
                         lambda u, s, e, r, v, f: (layer, e[u], 0, NC_UP + up_chunk(u, s, v))),
            pl.BlockSpec((1, 1, D_FF, F_DOWN),
                         lambda u, s, e, r, v, f: (layer, e[u], 0, down_chunk(u, s, v))),
            pl.BlockSpec((1, 1, 1, F_DOWN),
                         lambda u, s, e, r, v, f: (layer, e[u], 0, down_chunk(u, s, v))),
        ],
        out_specs=pl.BlockSpec((R_BLK, F_DOWN), lambda u, s, e, r, v, f: (f[u], down_chunk(u, s, v))),
        scratch_shapes=[
            pltpu.VMEM((R_BLK, D_MODEL), F32),
            pltpu.VMEM((R_BLK, D_MODEL), BF16),
            pltpu.VMEM((NC_ACT, R_BLK, F_ACT), BF16),
            pltpu.SemaphoreType.DMA(()),
        ],
    )
    return pl.pallas_call(
        _experts_kernel,
        grid_spec=grid_spec,
        out_shape=jax.ShapeDtypeStruct((N_BLK * R_BLK, D_MODEL), F32),
        compiler_params=_cparams("arbitrary", "arbitrary", vmem=VMEM_LIMIT_EXPERTS),
        name="moe_experts",
    )(eid, rows, valid, ueff, xs, w_up, w_up, b_up4, b_up4, w_down, b_down4)


def _combine_kernel(pos_hbm, y_hbm, w_ref, x1_ref, gate2_ref, gpost2_ref, o_ref,
                    pos_smem, gbuf, sem_pos, sem_rows, *, tile_off, sets, n_steps):
    i = pl.program_id(0)
    n_src = sets * TOP_K
    slots = (0, 1) if n_steps > 1 else (0,)

    def pos_copy(step, sl):
        return pltpu.make_async_copy(pos_hbm.at[pl.ds(step * sets + tile_off, sets)],
                                     pos_smem.at[sl], sem_pos.at[sl])

    def issue_rows(sl):
        for t in range(TOK_TILE):
            for k in range(n_src):
                src = pos_smem[sl, k // TOP_K, (k % TOP_K) * TOK_TILE + t]
                pltpu.make_async_copy(y_hbm.at[pl.ds(src, 1)], gbuf.at[sl, k, pl.ds(t, 1)],
                                      sem_rows.at[sl]).start(priority=k % 2)

    def finish(sl):
        for k in range(n_src):
            pltpu.make_async_copy(y_hbm.at[pl.ds(0, TOK_TILE)], gbuf.at[sl, k], sem_rows.at[sl]).wait()
        w = w_ref[...]
        m = w[:, 0:1] * gbuf[sl, 0]
        for k in range(1, n_src):
            m = m + w[:, k:k + 1] * gbuf[sl, k]
        o_ref[...] = x1_ref[...] + gate2_ref[0] * (_rms(m) * gpost2_ref[...])

    @pl.when(i == 0)
    def _():
        pos_copy(0, 0).start()
        pos_copy(0, 0).wait()
        issue_rows(0)
        if n_steps > 1:
            pos_copy(1, 1).start()

    for sl in slots:
        if n_steps > 1:
            @pl.when(jnp.logical_and(i % 2 == sl, i + 1 < n_steps))
            def _(sl=sl):
                pos_copy(i + 1, 1 - sl).wait()
                issue_rows(1 - sl)

            @pl.when(jnp.logical_and(i % 2 == sl, i + 2 < n_steps))
            def _(sl=sl):
                pos_copy(i + 2, sl).start()

        @pl.when(i % 2 == sl)
        def _(sl=sl):
            finish(sl)


def _combine(pos_tiles, y_sorted, wgt_t, x1, mod3, g_post2, rows_per_mod, tile_off):
    m = x1.shape[0]
    r = mod3.shape[1]
    sets = wgt_t.shape[1] // TOP_K
    mod_row = lambda i: i // (rows_per_mod // TOK_TILE)
    return pl.pallas_call(
        functools.partial(_combine_kernel, tile_off=tile_off, sets=sets, n_steps=m // TOK_TILE),
        grid=(m // TOK_TILE,),
        in_specs=[
            pl.BlockSpec(memory_space=pl.ANY),
            pl.BlockSpec(memory_space=pl.ANY),
            pl.BlockSpec((TOK_TILE, sets * TOP_K), lambda i: (i, 0)),
            pl.BlockSpec((TOK_TILE, D_MODEL), lambda i: (i, 0)),
            pl.BlockSpec((1, r, D_MODEL), lambda i: (mod_row(i), 0, 5)),
            pl.BlockSpec((1, D_MODEL), lambda i: (0, 0)),
        ],
        out_specs=pl.BlockSpec((TOK_TILE, D_MODEL), lambda i: (i, 0)),
        out_shape=jax.ShapeDtypeStruct((m, D_MODEL), F32),
        scratch_shapes=[
            pltpu.SMEM((2, sets, TOP_K * TOK_TILE), I32),
            pltpu.VMEM((2, sets * TOP_K, TOK_TILE, D_MODEL), F32),
            pltpu.SemaphoreType.DMA((2,)),
            pltpu.SemaphoreType.DMA((2,)),
        ],
        compiler_params=_cparams("arbitrary"),
        name="moe_combine",
    )(pos_tiles, y_sorted, wgt_t, x1, mod3, g_post2)


def _route_tables(counts, idx_all, rank_all):
    nblk = (counts + (R_BLK - 1)) // R_BLK
    cum = jnp.cumsum(nblk)
    base = cum - nblk
    total = cum[-1]
    u = jnp.arange(N_BLK, dtype=I32)
    valid = u < total
    ueff = jnp.minimum(u, jnp.maximum(total - 1, 0))
    eid = jnp.minimum(jnp.sum((cum[None, :] <= ueff[:, None]).astype(I32), axis=1), N_EXPERTS - 1)
    onehot_u = (eid[:, None] == jnp.arange(N_EXPERTS, dtype=I32)[None, :]).astype(I32)
    cnt_u = jnp.sum(onehot_u * counts[None, :], axis=1)
    base_u = jnp.sum(onehot_u * base[None, :], axis=1)
    rows = jnp.where(valid, jnp.clip(cnt_u - (ueff - base_u) * R_BLK, 0, R_BLK), 0)
    e_iota = jnp.arange(N_EXPERTS, dtype=I32)[None, None, :]
    base_tok = jnp.sum(jnp.where(idx_all[:, :, None] == e_iota, base[None, None, :], 0), axis=-1)
    pos = base_tok * R_BLK + rank_all
    n_tiles = idx_all.shape[1] // TOK_TILE
    pos_tiles = pos.reshape(TOP_K, n_tiles, TOK_TILE).transpose(1, 0, 2)
    pos_tiles = pos_tiles.reshape(n_tiles, TOP_K * TOK_TILE)
    return (eid.astype(I32), rows.astype(I32), valid.astype(I32), ueff.astype(I32)), pos_tiles.astype(I32)


def _layer_params(l, w):
    row = lambda a: a[l][None, :]
    return dict(
        g_pre1=row(w["g_pre1"]), g_post1=row(w["g_post1"]),
        g_pre2=row(w["g_pre2"]), g_post2=row(w["g_post2"]),
        w_gates_b=w["w_in"][l][:, N_REST:].astype(BF16),
        w_rest_b=w["w_in"][l][:, :N_REST].astype(BF16),
        w_pool_b=w["w_pool"][l].astype(BF16), b_pool=row(w["b_pool"]), pool_scale=row(w["pool_scale"]),
        ln_v_g=row(w["ln_v_g"]), ln_v_b=row(w["ln_v_b"]),
        w_s=w["w_s"][l], bs_full=jnp.repeat(w["b_s"][l].T, CHUNK, axis=1),
        ws0=jnp.repeat(w["w_s"][l][:, 0, 0], CHUNK)[None, :],
        bs0=jnp.repeat(w["b_s"][l][:, 0], CHUNK)[None, :],
        w_dw=w["w_dw"][l], b_dw=row(w["b_dw"]), ln_c_g=row(w["ln_c_g"]), ln_c_b=row(w["ln_c_b"]),
        w_pa_b=w["w_pa"][l].astype(BF16), w_pb_b=w["w_pb"][l].astype(BF16),
        w_pc_b=w["w_pc"][l].astype(BF16), w_o_b=w["w_o"][l].astype(BF16),
        w_rt=w["w_router"][l].T, w_rt_b=w["w_router"][l].T.astype(BF16),
        b_rt=w["b_router"][l][:, None],
    )


def kernel(x_prompt, x_sample, c_prompt, c_sample, state_pool, state_conv, w_ada, b_ada, g_pre1, g_post1, g_pre2, g_post2, w_in, b_in, w_pool, b_pool, pool_scale, ln_v_g, ln_v_b, w_s, b_s, w_dw, b_dw, ln_c_g, ln_c_b, w_pa, w_pb, w_pc, w_o, w_router, b_router, w_up, b_up, w_down, b_down):
    weights = dict(g_pre1=g_pre1, g_post1=g_post1, g_pre2=g_pre2, g_post2=g_post2, w_in=w_in,
                   w_pool=w_pool, b_pool=b_pool, pool_scale=pool_scale, ln_v_g=ln_v_g,
                   ln_v_b=ln_v_b, w_s=w_s, b_s=b_s, w_dw=w_dw, b_dw=b_dw, ln_c_g=ln_c_g,
                   ln_c_b=ln_c_b, w_pa=w_pa, w_pb=w_pb, w_pc=w_pc, w_o=w_o, w_router=w_router,
                   b_router=b_router)
    xp = x_prompt.reshape(N_PROMPT, D_MODEL)
    xs = x_sample.reshape(DEC_BATCH, D_MODEL)
    c_all = jnp.concatenate(
        [c_sample, c_prompt, jnp.zeros((N_MOD_PAD - N_MOD, D_MODEL), F32)], axis=0)
    b_ada3 = b_ada[:, None, :]
    b_in3 = b_in[:, None, :]
    b_gates3 = b_in[:, None, N_REST:]
    b_up4 = b_up[:, :, None, :]
    b_down4 = b_down[:, :, None, :]
    zero_cnt = jnp.zeros((N_EXPERTS, LANES), F32)

    pool_p, pool_s, conv_p, conv_s, v_s = [], [], [], [], []
    for l in range(DEPTH):
        p = _layer_params(l, weights)
        mod = _ada(c_all, w_ada, b_ada3, l)
        mod_s = mod[:DEC_BATCH][None]
        mod_p = mod[DEC_BATCH:N_MOD][:, None, :]

        gate_args = (p["g_pre1"], p["w_gates_b"], b_gates3, l, True)
        rest_args = (p["g_pre1"], p["w_rest_b"], b_in3, l, False)
        gates_p = _inproj(xp, mod_p, *gate_args, TM_IN, SEQ)
        zr_p = _inproj(xp, mod_p, *rest_args, TM_IN, SEQ)
        z_s = _inproj_sample(xs, mod_s, p["g_pre1"], w_in, b_in3, l)
        pa_p, sb_p, sc_p, npool, nconv = _mixer_prompt(zr_p, p)
        spool_t = jnp.transpose(state_pool[l], (1, 0, 2))
        sconv_t = jnp.transpose(state_conv[l], (1, 0, 2))
        pa_s, sb_s, sc_s, glu_s, vn_s = _mixer_sample(z_s, spool_t, sconv_t, w_pool, p, l)

        x1_p, hp_p, idx_p, wgt_p, rank_p, cnt_p = _merge(
            pa_p, sb_p, sc_p, gates_p, xp, mod_p, p, zero_cnt, TM_MERGE, SEQ)
        x1_s, hp_s, idx_s, wgt_s, rank_s, cnt_s = _merge_sample(
            pa_s, sb_s, sc_s, z_s, xs, mod_s, p, w_pa, w_pb, w_pc, w_o, cnt_p, l)
        idx_parts, rank_parts = [idx_p, idx_s], [rank_p, rank_s]
        wgt_s_t, cnt_last = wgt_s.T, cnt_s

        if l == 0:
            n = DEC_BATCH
            gates_m = _inproj(xs, mod_s, *gate_args, n, n)
            zr_m = _inproj(xs, mod_s, *rest_args, n, n)
            z_m = jnp.concatenate([zr_m[:, N_REST - W_A:], zr_m[:, :N_REST - W_A]], axis=1)
            pa_m, sb_m, sc_m, _, _ = _mixer_sample(z_m, spool_t, sconv_t, w_pool, p, l,
                                                   precise=False)
            _, _, idx_m, wgt_m, rank_m, cnt_last = _merge(
                pa_m.astype(BF16), sb_m.astype(BF16), sc_m.astype(BF16), gates_m, xs, mod_s, p,
                cnt_s, n, n)
            idx_parts.append(idx_m)
            rank_parts.append(rank_m)
            wgt_s_t = 0.5 * jnp.concatenate([wgt_s.T, wgt_m.T], axis=1)

        counts = cnt_last[:, 0].astype(I32)
        idx_all = jnp.concatenate(idx_parts, axis=1)
        rank_all = jnp.concatenate(rank_parts, axis=1)
        meta, pos_tiles = _route_tables(counts, idx_all, rank_all)
        x_sorted = _dispatch(pos_tiles, hp_p, hp_s)
        y_sorted = _experts(meta, x_sorted, w_up, b_up4, w_down, b_down4, l)
        xp = _combine(pos_tiles, y_sorted, wgt_p.T, x1_p, mod_p, p["g_post2"], SEQ, 0)
        xs = _combine(pos_tiles, y_sorted, wgt_s_t, x1_s, mod_s, p["g_post2"], DEC_BATCH,
                      N_PROMPT // TOK_TILE)

        pool_p.append(npool[:, POOL_HALO - POOL_BUF:])
        conv_p.append(nconv[:, CONV_HALO - CONV_BUF:])
        pool_s.append(jnp.concatenate([state_pool[l][:, 1:], z_s[:, None, :W_A]], axis=1))
        conv_s.append(jnp.concatenate([state_conv[l][:, 1:], glu_s[:, None, :]], axis=1))
        v_s.append(vn_s[:, None, :])

    return (xp.reshape(BATCH, SEQ, D_MODEL), xs.reshape(DEC_BATCH, 1, D_MODEL),
            jnp.stack(pool_p), jnp.stack(pool_s), jnp.stack(conv_p), jnp.stack(conv_s),
            jnp.stack(v_s))
```
